```python
import math
import jax
import jax.numpy as jnp
from jax import lax
import numpy as np

D_MODEL = 1024
BATCH = 2
SEQ = 8192
DEPTH = 4
DEC_BATCH = 128
DEC_SEQ = 8
PAST_LEN = 8192
PAGE_SIZE = 128

N_MIXERS = 3
FFN_HIDDEN = -(-8 * D_MODEL // (3 * 256)) * 256

MLA_HEADS = D_MODEL // 128
MLA_NOPE = 128
MLA_ROPE = 64
MLA_V = 128
MLA_Q_RANK = 3 * D_MODEL // 8
MLA_KV_RANK = D_MODEL // 4
ROPE_THETA = 10000.0

DA_HEAD_DIM = 64
DA_HEADS = D_MODEL // (2 * DA_HEAD_DIM)
DA_KV_HEADS = DA_HEADS // 2

DL_CONFIGS = ((128, 1), (512, 4), (2048, 16))
DL_HEAD_DIM = 128
DL_HEADS = D_MODEL // DL_HEAD_DIM

Q_BLOCK = 128
RMS_EPS = 1e-6
NEG_INF = -1e30

kernel_name = 'hybrid_mla_diff_dilated_decoder_step'


def _rmsnorm(x, g):
    xf = x.astype(jnp.float32)
    y = xf * lax.rsqrt(jnp.mean(xf * xf, axis=-1, keepdims=True) + RMS_EPS)
    return (y * g.astype(jnp.float32)).astype(x.dtype)


def _adaln(c, w, b):
    mod = (jax.nn.silu(c) @ w + b).astype(c.dtype)
    return [m[:, None, :] for m in jnp.split(mod, 6, axis=-1)]


def _modulate(h, shift, scale):
    return h * (1 + scale) + shift


def _swiglu(h, w1, w3, w2):
    return (jax.nn.silu(h @ w1) * (h @ w3)) @ w2


def _alibi_slopes(n):
    return jnp.asarray(np.array([2.0 ** (-8.0 * (i + 1) / n) for i in range(n)], dtype=np.float32))


def _alibi(slopes, q_pos, k_pos):
    dist = (q_pos[:, None] - k_pos[None, :]).astype(jnp.float32)
    return -slopes.astype(jnp.float32)[:, None, None] * dist[None]


def _rope(x, pos):
    half = x.shape[-1] // 2
    freqs = jnp.power(ROPE_THETA, -jnp.arange(half, dtype=jnp.float32) / half)
    ang = pos.astype(jnp.float32)[:, None] * freqs[None, :]
    shape = (1, x.shape[1]) + (1,) * (x.ndim - 3) + (half,)
    cos, sin = jnp.cos(ang).reshape(shape), jnp.sin(ang).reshape(shape)
    xf = x.astype(jnp.float32)
    x1, x2 = xf[..., :half], xf[..., half:]
    return jnp.concatenate([x1 * cos - x2 * sin, x1 * sin + x2 * cos], axis=-1).astype(x.dtype)


def _causal_attention_blocked(q, k, v, slopes, scale):
    B, T, N, Dk = q.shape
    Dv = v.shape[-1]
    nq = T // Q_BLOCK
    kf, vf = k.astype(jnp.float32), v.astype(jnp.float32)
    k_pos = jnp.arange(T)
    qb = jnp.transpose(q.astype(jnp.float32).reshape(B, nq, Q_BLOCK, N, Dk), (1, 0, 2, 3, 4))

    def one_block(args):
        i, qi = args
        q_pos = i * Q_BLOCK + jnp.arange(Q_BLOCK)
        s = jnp.einsum('bqnd,bknd->bnqk', qi, kf) * scale
        if slopes is not None:
            s = s + _alibi(slopes, q_pos, k_pos)[None]
        s = jnp.where(k_pos[None, :] <= q_pos[:, None], s, NEG_INF)
        return jnp.einsum('bnqk,bknd->bqnd', jax.nn.softmax(s, axis=-1), vf)

    o = lax.map(one_block, (jnp.arange(nq), qb))
    return jnp.transpose(o, (1, 0, 2, 3, 4)).reshape(B, T, N, Dv)


def _paged_causal_attention(q, fetch_page, n_pages, k_new, v_new, slopes, scale, past_len):
    DB, S, N, _ = q.shape
    Dv = v_new.shape[-1]
    qf = q.astype(jnp.float32)
    q_pos = past_len + jnp.arange(S)

    def logits(k, k_pos):
        s = jnp.einsum('bqnd,bknd->bnqk', qf, k.astype(jnp.float32)) * scale
        if slopes is not None:
            s = s + _alibi(slopes, q_pos, k_pos)[None]
        return s

    def absorb(carry, s, v):
        m, l, acc = carry
        m_new = jnp.maximum(m, jnp.max(s, axis=-1))
        corr = jnp.exp(m - m_new)
        pr = jnp.exp(s - m_new[..., None])
        acc = acc * corr[..., None] + jnp.einsum('bnqk,bknd->bnqd', pr, v.astype(jnp.float32))
        return (m_new, l * corr + jnp.sum(pr, axis=-1), acc)

    def step(carry, p):
        k, v = fetch_page(p)
        k_pos = p * k.shape[1] + jnp.arange(k.shape[1])
        return absorb(carry, logits(k, k_pos), v), None

    init = (jnp.full((DB, N, S), NEG_INF, jnp.float32),
            jnp.zeros((DB, N, S), jnp.float32),
            jnp.zeros((DB, N, S, Dv), jnp.float32))
    carry, _ = lax.scan(step, init, jnp.arange(n_pages))
    causal = jnp.arange(S)[None, :] <= jnp.arange(S)[:, None]
    s_new = jnp.where(causal, logits(k_new, q_pos), NEG_INF)
    _, l, acc = absorb(carry, s_new, v_new)
    return jnp.transpose(acc / l[..., None], (0, 2, 1, 3))


def _mla_gain(g):
    pe = g[MLA_NOPE:]
    return jnp.concatenate([g[:MLA_NOPE], pe, pe])


def _mla_project(h, pos, w_dqkv, g_q, g_kv, w_uq, q_norm):
    down = h @ w_dqkv
    cq = _rmsnorm(down[..., :MLA_Q_RANK], g_q)
    ckv = _rmsnorm(down[..., MLA_Q_RANK:MLA_Q_RANK + MLA_KV_RANK], g_kv)
    kpe = _rope(down[..., MLA_Q_RANK + MLA_KV_RANK:], pos)
    q = jnp.einsum('bsr,rhd->bshd', cq, w_uq)
    q = jnp.concatenate([q[..., :MLA_NOPE], _rope(q[..., MLA_NOPE:], pos)], axis=-1)
    return _rmsnorm(q, _mla_gain(q_norm)), ckv, kpe


def _mla_keys(ckv, kpe, w_uk, k_norm):
    k_nope = jnp.einsum('bsr,rhd->bshd', ckv, w_uk)
    k_pe = jnp.broadcast_to(kpe[:, :, None, :], k_nope.shape[:3] + (MLA_ROPE,)).astype(k_nope.dtype)
    return _rmsnorm(jnp.concatenate([k_nope, k_pe], axis=-1), _mla_gain(k_norm))


def _mla_layer(hp, hs, caches, page_table, w_dqkv, g_q, g_kv, w_uq, w_uk, w_uv, q_norm, k_norm, w_o):
    cache_lat, cache_kpe = caches
    scale = (MLA_NOPE + MLA_ROPE) ** -0.5
    B, T, _ = hp.shape
    q, ckv, kpe = _mla_project(hp, jnp.arange(T), w_dqkv, g_q, g_kv, w_uq, q_norm)
    k = _mla_keys(ckv, kpe, w_uk, k_norm)
    v = jnp.einsum('bsr,rhd->bshd', ckv, w_uv)
    o = _causal_attention_blocked(q, k, v, None, scale)
    out_p = o.reshape(B, T, MLA_HEADS * MLA_V).astype(hp.dtype) @ w_o
    DB, S, _ = hs.shape
    n_pages = page_table.shape[1]
    past = n_pages * cache_lat.shape[1]
    qs, ckv_s, kpe_s = _mla_project(hs, past + jnp.arange(S), w_dqkv, g_q, g_kv, w_uq, q_norm)

    def heads(lat):
        return jnp.broadcast_to(lat[:, :, None, :], lat.shape[:2] + (MLA_HEADS, MLA_KV_RANK))

    def fetch(p):
        phys = page_table[:, p]
        lat = cache_lat[phys]
        return _mla_keys(lat, cache_kpe[phys], w_uk, k_norm), heads(lat)

    o_lat = _paged_causal_attention(qs, fetch, n_pages, _mla_keys(ckv_s, kpe_s, w_uk, k_norm),
                                    heads(ckv_s), None, scale, past)
    o_s = jnp.einsum('bshr,rhd->bshd', o_lat, w_uv.astype(jnp.float32))
    out_s = o_s.reshape(DB, S, MLA_HEADS * MLA_V).astype(hs.dtype) @ w_o
    return out_p, out_s, (ckv, ckv_s, kpe, kpe_s)


def _diff_project(h, w_qkv, q_norm, k_norm):
    B, S, _ = h.shape
    nq = DA_HEADS * 2 * DA_HEAD_DIM
    nk = DA_KV_HEADS * 2 * DA_HEAD_DIM
    qkv = h @ w_qkv
    q = qkv[..., :nq].reshape(B, S, DA_HEADS, 2, DA_HEAD_DIM)
    k = qkv[..., nq:nq + nk].reshape(B, S, DA_KV_HEADS, 2, DA_HEAD_DIM)
    v = qkv[..., nq + nk:].reshape(B, S, DA_KV_HEADS, 2 * DA_HEAD_DIM)
    return _rmsnorm(q, q_norm), _rmsnorm(k, k_norm), v


def _diff_expand(k, v):
    G = DA_HEADS // DA_KV_HEADS
    B, S = k.shape[:2]
    kx = jnp.repeat(k, G, axis=2).reshape(B, S, DA_HEADS * 2, DA_HEAD_DIM)
    vx = jnp.broadcast_to(jnp.repeat(v, G, axis=2)[:, :, :, None, :],
                          (B, S, DA_HEADS, 2, 2 * DA_HEAD_DIM)).reshape(B, S, DA_HEADS * 2, 2 * DA_HEAD_DIM)
    return kx, vx


def _diff_merge(o, lam, lam_init, subln, w_o, dtype):
    B, S = o.shape[:2]
    o = o.reshape(B, S, DA_HEADS, 2, 2 * DA_HEAD_DIM)
    d = o[:, :, :, 0] - lam * o[:, :, :, 1]
    d = _rmsnorm(d, subln) * (1.0 - lam_init)
    return d.reshape(B, S, DA_HEADS * 2 * DA_HEAD_DIM).astype(dtype) @ w_o


def _diff_layer(hp, hs, caches, page_table, layer_idx, w_qkv, q_norm, k_norm,
                lambda_q1, lambda_k1, lambda_q2, lambda_k2, subln, w_o):
    cache_k, cache_v = caches
    lam_init = 0.8 - 0.6 * math.exp(-0.3 * layer_idx)
    lam = (jnp.exp(jnp.sum(lambda_q1.astype(jnp.float32) * lambda_k1.astype(jnp.float32)))
           - jnp.exp(jnp.sum(lambda_q2.astype(jnp.float32) * lambda_k2.astype(jnp.float32))) + lam_init)
    slopes = jnp.repeat(_alibi_slopes(DA_HEADS), 2)
    scale = DA_HEAD_DIM ** -0.5
    B, T, _ = hp.shape
    q, k, v = _diff_project(hp, w_qkv, q_norm, k_norm)
    kx, vx = _diff_expand(k, v)
    o = _causal_attention_blocked(q.reshape(B, T, 2 * DA_HEADS, DA_HEAD_DIM), kx, vx, slopes, scale)
    out_p = _diff_merge(o, lam, lam_init, subln, w_o, hp.dtype)
    DB, S, _ = hs.shape
    n_pages = page_table.shape[1]
    past = n_pages * cache_k.shape[1]
    qs, ks, vs = _diff_project(hs, w_qkv, q_norm, k_norm)

    def fetch(p):
        phys = page_table[:, p]
        return _diff_expand(cache_k[phys], cache_v[phys])

    kxs, vxs = _diff_expand(ks, vs)
    o_s = _paged_causal_attention(qs.reshape(DB, S, 2 * DA_HEADS, DA_HEAD_DIM), fetch, n_pages,
                                  kxs, vxs, slopes, scale, past)
    out_s = _diff_merge(o_s, lam, lam_init, subln, w_o, hs.dtype)
    return out_p, out_s, (k, ks, v, vs)


def _dil_project(h, w_qkv, q_norm, k_norm):
    B, S, _ = h.shape
    qkv = (h @ w_qkv).reshape(B, S, len(DL_CONFIGS), 3, DL_HEADS, DL_HEAD_DIM)
    return _rmsnorm(qkv[:, :, :, 0], q_norm), _rmsnorm(qkv[:, :, :, 1], k_norm), qkv[:, :, :, 2]


def _dilated_band_attention(q, k, v, dil, n_back, slopes, scale):
    B, T, H, Dh = q.shape
    L = T // dil
    nb = n_back
    nblk = -(-L // nb)
    Lp = nblk * nb

    def sub(x):
        return jnp.transpose(x.reshape(B, L, dil, H, Dh), (0, 2, 1, 3, 4))

    def band(x):
        xp = jnp.pad(sub(x), ((0, 0), (0, 0), (nb, Lp - L), (0, 0), (0, 0)))
        prev = xp[:, :, :Lp].reshape(B, dil, nblk, nb, H, Dh)
        cur = xp[:, :, nb:].reshape(B, dil, nblk, nb, H, Dh)
        return jnp.concatenate([prev, cur], axis=3)

    qb = jnp.pad(sub(q), ((0, 0), (0, 0), (0, Lp - L), (0, 0), (0, 0))).reshape(B, dil, nblk, nb, H, Dh)
    kb, vb = band(k), band(v)
    s = jnp.einsum('brnqhd,brnkhd->brnhqk', qb.astype(jnp.float32), kb.astype(jnp.float32)) * scale
    a = jnp.arange(nb)[:, None]
    bk = jnp.arange(2 * nb)[None, :]
    steps = a - bk + nb
    key_j = jnp.arange(nblk)[:, None, None] * nb + bk[None] - nb
    valid = (steps >= 0) & (steps <= nb) & (key_j >= 0)
    bias = -slopes[:, None, None] * (steps * dil).astype(jnp.float32)[None]
    s = jnp.where(valid[None, None, :, None], s + bias[None, None, None], NEG_INF)
    lse = jax.nn.logsumexp(s, axis=-1)
    o = jnp.einsum('brnhqk,brnkhd->brnqhd', jnp.exp(s - lse[..., None]), vb.astype(jnp.float32))
    o = jnp.transpose(o.reshape(B, dil, Lp, H, Dh)[:, :, :L], (0, 2, 1, 3, 4)).reshape(B, T, H, Dh)
    lse = jnp.transpose(lse, (0, 1, 2, 4, 3)).reshape(B, dil, Lp, H)[:, :, :L]
    lse = jnp.transpose(lse, (0, 2, 1, 3)).reshape(B, T, H)
    return o, lse


def _dilated_window_sample(q, k_new, v_new, buf, dil, n_back, slopes, scale):
    Wb = buf.shape[1]
    S = q.shape[1]
    kv_all = jnp.concatenate([buf, jnp.stack([k_new, v_new], axis=2).astype(buf.dtype)], axis=1)
    steps = jnp.arange(n_back + 1)
    idx = Wb + jnp.arange(S)[:, None] - steps[None, :] * dil
    valid = idx >= 0
    kv = kv_all[:, jnp.maximum(idx, 0)].astype(jnp.float32)
    s = jnp.einsum('bqhd,bqkhd->bhqk', q.astype(jnp.float32), kv[:, :, :, 0]) * scale
    s = s - slopes[:, None, None] * (steps * dil).astype(jnp.float32)
    s = jnp.where(valid[None, None], s, NEG_INF)
    lse = jax.nn.logsumexp(s, axis=-1)
    o = jnp.einsum('bhqk,bqkhd->bqhd', jnp.exp(s - lse[..., None]), kv[:, :, :, 1])
    return o, jnp.transpose(lse, (0, 2, 1)), kv_all[:, S:]


def _merge_groups(outs, lses):
    w = jax.nn.softmax(jnp.stack(lses, axis=0), axis=0)
    return jnp.sum(w[..., None] * jnp.stack(outs, axis=0), axis=0)


def _dilated_layer(hp, hs, caches, w_qkv, q_norm, k_norm, w_o):
    slopes = _alibi_slopes(DL_HEADS)
    scale = DL_HEAD_DIM ** -0.5
    B, T, _ = hp.shape
    q, k, v = _dil_project(hp, w_qkv, q_norm, k_norm)
    outs, lses, st_p = [], [], []
    for g, (win, dil) in enumerate(DL_CONFIGS):
        o, lse = _dilated_band_attention(q[:, :, g], k[:, :, g], v[:, :, g], dil, win // dil, slopes, scale)
        outs.append(o)
        lses.append(lse)
        wb = min(win, T)
        st_p.append(jnp.stack([k[:, T - wb:, g], v[:, T - wb:, g]], axis=2))
    out_p = _merge_groups(outs, lses).reshape(B, T, DL_HEADS * DL_HEAD_DIM).astype(hp.dtype) @ w_o
    DB, S, _ = hs.shape
    qs, ks, vs = _dil_project(hs, w_qkv, q_norm, k_norm)
    outs, lses, st_s = [], [], []
    for g, (win, dil) in enumerate(DL_CONFIGS):
        o, lse, new_buf = _dilated_window_sample(qs[:, :, g], ks[:, :, g], vs[:, :, g], caches[g],
                                                 dil, win // dil, slopes, scale)
        outs.append(o)
        lses.append(lse)
        st_s.append(new_buf)
    out_s = _merge_groups(outs, lses).reshape(DB, S, DL_HEADS * DL_HEAD_DIM).astype(hs.dtype) @ w_o
    return out_p, out_s, (st_p[0], st_s[0], st_p[1], st_s[1], st_p[2], st_s[2])


def setup_inputs(seed: int = 0) -> dict:
    key = jax.random.key(seed)
    keys = iter(jax.random.split(key, 128))

    def nrm(shape, scale=1.0):
        return jax.random.normal(next(keys), shape, jnp.float32) * scale

    def gain(n):
        return 1.0 + nrm((n,), 0.02)

    d = D_MODEL
    n_pages = PAST_LEN // PAGE_SIZE
    n_used = DEC_BATCH * n_pages
    n_pool = n_used + n_used // 4
    inp = {}
    inp['x_prompt'] = nrm((BATCH, SEQ, d))
    inp['x_sample'] = nrm((DEC_BATCH, DEC_SEQ, d))
    inp['cache_l0_latent'] = nrm((n_pool, PAGE_SIZE, MLA_KV_RANK))
    inp['cache_l0_kpe'] = nrm((n_pool, PAGE_SIZE, MLA_ROPE))
    inp['cache_l1_k'] = nrm((n_pool, PAGE_SIZE, DA_KV_HEADS, 2, DA_HEAD_DIM))
    inp['cache_l1_v'] = nrm((n_pool, PAGE_SIZE, DA_KV_HEADS, 2 * DA_HEAD_DIM))
    for win, _ in DL_CONFIGS:
        inp[f'cache_l2_kv_w{win}'] = nrm((DEC_BATCH, min(win, PAST_LEN), 2, DL_HEADS, DL_HEAD_DIM))
    inp['cache_l3_latent'] = nrm((n_pool, PAGE_SIZE, MLA_KV_RANK))
    inp['cache_l3_kpe'] = nrm((n_pool, PAGE_SIZE, MLA_ROPE))
    inp['page_table'] = jax.random.permutation(next(keys), n_pool)[:n_used].reshape(
        DEC_BATCH, n_pages).astype(jnp.int32)
    inp['c_prompt'] = nrm((BATCH, d))
    inp['c_sample'] = nrm((DEC_BATCH, d))
    for i in range(DEPTH):
        p = f'l{i}_'
        kind = i % N_MIXERS
        inp[p + 'ada_w'] = nrm((d, 6 * d), 0.5 * d ** -0.5)
        inp[p + 'ada_b'] = nrm((6 * d,), 0.01)
        inp[p + 'norm_mix'] = gain(d)
        if kind == 0:
            inp[p + 'w_dqkv'] = nrm((d, MLA_Q_RANK + MLA_KV_RANK + MLA_ROPE), d ** -0.5)
            inp[p + 'g_q'] = gain(MLA_Q_RANK)
            inp[p + 'g_kv'] = gain(MLA_KV_RANK)
            inp[p + 'w_uq'] = nrm((MLA_Q_RANK, MLA_HEADS, MLA_NOPE + MLA_ROPE), MLA_Q_RANK ** -0.5)
            inp[p + 'w_uk'] = nrm((MLA_KV_RANK, MLA_HEADS, MLA_NOPE), MLA_KV_RANK ** -0.5)
            inp[p + 'w_uv'] = nrm((MLA_KV_RANK, MLA_HEADS, MLA_V), MLA_KV_RANK ** -0.5)
            inp[p + 'q_norm'] = gain(MLA_NOPE + MLA_ROPE // 2)
            inp[p + 'k_norm'] = gain(MLA_NOPE + MLA_ROPE // 2)
            inp[p + 'w_o'] = nrm((MLA_HEADS * MLA_V, d), (MLA_HEADS * MLA_V) ** -0.5)
        elif kind == 1:
            n_out = DA_HEADS * 2 * DA_HEAD_DIM + 2 * DA_KV_HEADS * 2 * DA_HEAD_DIM
            inp[p + 'w_qkv'] = nrm((d, n_out), d ** -0.5)
            inp[p + 'q_norm'] = gain(DA_HEAD_DIM)
            inp[p + 'k_norm'] = gain(DA_HEAD_DIM)
            inp[p + 'lambda_q1'] = nrm((DA_HEAD_DIM,), 0.1)
            inp[p + 'lambda_k1'] = nrm((DA_HEAD_DIM,), 0.1)
            inp[p + 'lambda_q2'] = nrm((DA_HEAD_DIM,), 0.1)
            inp[p + 'lambda_k2'] = nrm((DA_HEAD_DIM,), 0.1)
            inp[p + 'subln'] = gain(2 * DA_HEAD_DIM)
            inp[p + 'w_o'] = nrm((DA_HEADS * 2 * DA_HEAD_DIM, d), (DA_HEADS * 2 * DA_HEAD_DIM) ** -0.5)
        else:
            inp[p + 'w_qkv'] = nrm((d, len(DL_CONFIGS) * 3 * DL_HEADS * DL_HEAD_DIM), d ** -0.5)
            inp[p + 'q_norm'] = gain(DL_HEAD_DIM)
            inp[p + 'k_norm'] = gain(DL_HEAD_DIM)
            inp[p + 'w_o'] = nrm((DL_HEADS * DL_HEAD_DIM, d), (DL_HEADS * DL_HEAD_DIM) ** -0.5)
        inp[p + 'norm_ffn'] = gain(d)
        inp[p + 'ffn_w1'] = nrm((d, FFN_HIDDEN), d ** -0.5)
        inp[p + 'ffn_w3'] = nrm((d, FFN_HIDDEN), d ** -0.5)
        inp[p + 'ffn_w2'] = nrm((FFN_HIDDEN, d), FFN_HIDDEN ** -0.5)
    return inp


def reference(x_prompt, x_sample, cache_l0_latent, cache_l0_kpe, cache_l1_k, cache_l1_v,
              cache_l2_kv_w128, cache_l2_kv_w512, cache_l2_kv_w2048, cache_l3_latent, cache_l3_kpe,
              page_table, c_prompt, c_sample,
              l0_ada_w, l0_ada_b, l0_norm_mix, l0_w_dqkv, l0_g_q, l0_g_kv, l0_w_uq, l0_w_uk, l0_w_uv,
              l0_q_norm, l0_k_norm, l0_w_o, l0_norm_ffn, l0_ffn_w1, l0_ffn_w3, l0_ffn_w2,
              l1_ada_w, l1_ada_b, l1_norm_mix, l1_w_qkv, l1_q_norm, l1_k_norm, l1_lambda_q1, l1_lambda_k1,
              l1_lambda_q2, l1_lambda_k2, l1_subln, l1_w_o, l1_norm_ffn, l1_ffn_w1, l1_ffn_w3, l1_ffn_w2,
              l2_ada_w, l2_ada_b, l2_norm_mix, l2_w_qkv, l2_q_norm, l2_k_norm, l2_w_o,
              l2_norm_ffn, l2_ffn_w1, l2_ffn_w3, l2_ffn_w2,
              l3_ada_w, l3_ada_b, l3_norm_mix, l3_w_dqkv, l3_g_q, l3_g_kv, l3_w_uq, l3_w_uk, l3_w_uv,
              l3_q_norm, l3_k_norm, l3_w_o, l3_norm_ffn, l3_ffn_w1, l3_ffn_w3, l3_ffn_w2):
    block = [
        (l0_ada_w, l0_ada_b, l0_norm_mix, l0_norm_ffn, l0_ffn_w1, l0_ffn_w3, l0_ffn_w2),
        (l1_ada_w, l1_ada_b, l1_norm_mix, l1_norm_ffn, l1_ffn_w1, l1_ffn_w3, l1_ffn_w2),
        (l2_ada_w, l2_ada_b, l2_norm_mix, l2_norm_ffn, l2_ffn_w1, l2_ffn_w3, l2_ffn_w2),
        (l3_ada_w, l3_ada_b, l3_norm_mix, l3_norm_ffn, l3_ffn_w1, l3_ffn_w3, l3_ffn_w2),
    ]
    mixer = [
        (l0_w_dqkv, l0_g_q, l0_g_kv, l0_w_uq, l0_w_uk, l0_w_uv, l0_q_norm, l0_k_norm, l0_w_o),
        (l1_w_qkv, l1_q_norm, l1_k_norm, l1_lambda_q1, l1_lambda_k1, l1_lambda_q2, l1_lambda_k2,
         l1_subln, l1_w_o),
        (l2_w_qkv, l2_q_norm, l2_k_norm, l2_w_o),
        (l3_w_dqkv, l3_g_q, l3_g_kv, l3_w_uq, l3_w_uk, l3_w_uv, l3_q_norm, l3_k_norm, l3_w_o),
    ]
    caches = [
        (cache_l0_latent, cache_l0_kpe),
        (cache_l1_k, cache_l1_v),
        (cache_l2_kv_w128, cache_l2_kv_w512, cache_l2_kv_w2048),
        (cache_l3_latent, cache_l3_kpe),
    ]
    xp, xs = x_prompt, x_sample
    new_states = []
    for i in range(DEPTH):
        ada_w, ada_b, g_mix, g_ffn, w1, w3, w2 = block[i]
        mod_p = _adaln(c_prompt, ada_w, ada_b)
        mod_s = _adaln(c_sample, ada_w, ada_b)
        hp = _modulate(_rmsnorm(xp, g_mix), mod_p[0], mod_p[1])
        hs = _modulate(_rmsnorm(xs, g_mix), mod_s[0], mod_s[1])
        kind = i % N_MIXERS
        if kind == 0:
            op, os_, st = _mla_layer(hp, hs, caches[i], page_table, *mixer[i])
        elif kind == 1:
            op, os_, st = _diff_layer(hp, hs, caches[i], page_table, i, *mixer[i])
        else:
            op, os_, st = _dilated_layer(hp, hs, caches[i], *mixer[i])
        xp = xp + mod_p[2] * op
        xs = xs + mod_s[2] * os_
        hp = _modulate(_rmsnorm(xp, g_ffn), mod_p[3], mod_p[4])
        hs = _modulate(_rmsnorm(xs, g_ffn), mod_s[3], mod_s[4])
        xp = xp + mod_p[5] * _swiglu(hp, w1, w3, w2)
        xs = xs + mod_s[5] * _swiglu(hs, w1, w3, w2)
        new_states.append(st)
    l0_lat_p, l0_lat_s, l0_kpe_p, l0_kpe_s = new_states[0]
    l1_k_p, l1_k_s, l1_v_p, l1_v_s = new_states[1]
    l2_w128_p, l2_w128_s, l2_w512_p, l2_w512_s, l2_w2048_p, l2_w2048_s = new_states[2]
    l3_lat_p, l3_lat_s, l3_kpe_p, l3_kpe_s = new_states[3]
    return (xp, xs, l0_lat_p, l0_lat_s, l0_kpe_p, l0_kpe_s, l1_k_p, l1_k_s, l1_v_p, l1_v_s,
            l2_w128_p, l2_w128_s, l2_w512_p, l2_w512_s, l2_w2048_p, l2_w2048_s,
            l3_lat_p, l3_lat_s, l3_kpe_p, l3_kpe_s)
```

```python
import functools
import math

import numpy as np
import jax
import jax.numpy as jnp
from jax import lax
from jax.experimental import pallas as pl
from jax.experimental.pallas import tpu as pltpu

F32 = jnp.float32
BF16 = jnp.bfloat16

RMS_EPS = 1e-6
NEG_INF = -1e30
ROPE_THETA = 10000.0

MLA_HEADS = 8
MLA_NOPE = 128
MLA_ROPE = 64
MLA_Q_RANK = 384
MLA_KV_RANK = 256
DA_HEAD_DIM = 64
DA_HEADS = 8
DA_KV_HEADS = 4
DL_CONFIGS = ((128, 1), (512, 4), (2048, 16))
DL_HEADS = 8
DL_HEAD_DIM = 128
N_BACK = 128

LANES = 128
VMEM_LIMIT = 48 * 1024 * 1024


def _cparams(sem):
    return pltpu.CompilerParams(dimension_semantics=sem, vmem_limit_bytes=VMEM_LIMIT)


def _dot(a, b):
    return jnp.dot(a, b, preferred_element_type=F32)


def _dot_t(a, b):
    return lax.dot_general(a, b, (((1,), (1,)), ((), ())), preferred_element_type=F32)


def _norm_mod(x, g, shift, scale):
    ms = jnp.mean(x * x, axis=-1, keepdims=True)
    y = x * lax.rsqrt(ms + RMS_EPS) * g
    return y * (1.0 + scale) + shift


def _rms_scale(sumsq, n):
    return lax.rsqrt(sumsq * (1.0 / n) + RMS_EPS)


def _tok_tiles(x, prompt_tile):
    nb, r, _ = x.shape
    if r >= prompt_tile:
        return 1, prompt_tile
    return min(nb, 128), r


def _adaln_body(c_ref, w_ref, b_ref, o_ref):
    c = c_ref[...]
    a = (c * jax.nn.sigmoid(c)).astype(BF16)
    o_ref[0] = _dot(a, w_ref[...].astype(BF16)) + b_ref[0]


def _adaln(c, w, b):
    rows, d = c.shape
    n = w.shape[1] // d
    return pl.pallas_call(
        _adaln_body,
        grid=(n,),
        in_specs=[pl.BlockSpec((rows, d), lambda j: (0, 0)),
                  pl.BlockSpec((d, d), lambda j: (0, j)),
                  pl.BlockSpec((1, 1, d), lambda j: (j, 0, 0))],
        out_specs=pl.BlockSpec((1, rows, d), lambda j: (j, 0, 0)),
        out_shape=jax.ShapeDtypeStruct((n, rows, d), F32),
        compiler_params=_cparams(("arbitrary",)),
        name="adaln",
    )(c, w, b.reshape(n, 1, d))


def _group_proj_body(x_ref, g_ref, sh_ref, sc_ref, w_ref, gn_ref, o_ref, h_scr, *, group, tile_kind):
    bb, tt, d = x_ref.shape
    rows = bb * tt
    tn = w_ref.shape[1]
    j = pl.program_id(2)

    @pl.when(j == 0)
    def _():
        h = _norm_mod(x_ref[...], g_ref[...], sh_ref[...], sc_ref[...])
        h_scr[...] = h.reshape(rows, d).astype(BF16)

    y = _dot(h_scr[...], w_ref[...])
    raw = tile_kind(j)

    @pl.when(raw)
    def _():
        o_ref[...] = y.reshape(bb, tt, tn)

    @pl.when(jnp.logical_not(raw))
    def _():
        gn = gn_ref[0]
        lane = lax.broadcasted_iota(jnp.int32, (rows, LANES), 1)
        parts = []
        for c in range(tn // LANES):
            yc = y[:, LANES * c:LANES * (c + 1)]
            sq = yc * yc
            if group == LANES:
                r = _rms_scale(jnp.sum(sq, axis=-1, keepdims=True), group)
            else:
                lo = jnp.sum(jnp.where(lane < group, sq, 0.0), axis=-1, keepdims=True)
                hi = jnp.sum(jnp.where(lane >= group, sq, 0.0), axis=-1, keepdims=True)
                r = jnp.where(lane < group, _rms_scale(lo, group), _rms_scale(hi, group))
            parts.append(yc * r)
        yn = jnp.concatenate(parts, axis=-1) * gn
        o_ref[...] = yn.reshape(bb, tt, tn)


def _group_proj(x, g, shift, scale, w_bf, gains, gain_idx, tile_kind, group, tn, prompt_tile):
    nb, r, d = x.shape
    n = w_bf.shape[1]
    bb, tt = _tok_tiles(x, prompt_tile)
    body = functools.partial(_group_proj_body, group=group, tile_kind=tile_kind)
    return pl.pallas_call(
        body,
        grid=(nb // bb, r // tt, n // tn),
        in_specs=[pl.BlockSpec((bb, tt, d), lambda b, t, j: (b, t, 0)),
                  pl.BlockSpec((1, 1, d), lambda b, t, j: (0, 0, 0)),
                  pl.BlockSpec((bb, 1, d), lambda b, t, j: (b, 0, 0)),
                  pl.BlockSpec((bb, 1, d), lambda b, t, j: (b, 0, 0)),
                  pl.BlockSpec((d, tn), lambda b, t, j: (0, j)),
                  pl.BlockSpec((1, 1, tn), lambda b, t, j: (gain_idx(j), 0, 0))],
        out_specs=pl.BlockSpec((bb, tt, tn), lambda b, t, j: (b, t, j)),
        out_shape=jax.ShapeDtypeStruct((nb, r, n), F32),
        scratch_shapes=[pltpu.VMEM((bb * tt, d), BF16)],
        compiler_params=_cparams(("parallel", "parallel", "arbitrary")),
        name="group_proj",
    )(x, g.reshape(1, 1, d), shift, scale, w_bf, gains)


def _mla_proj_body(x_ref, g_ref, sh_ref, sc_ref, cos_ref, sin_ref, wd_ref, gq_ref, gkv_ref,
                   wuq_ref, wuk_ref, wuv_ref, qn_ref, qp_ref, kn_ref, kp_ref, *outs, sample):
    bb, tt, d = x_ref.shape
    rows = bb * tt
    nh, nope = MLA_HEADS, MLA_NOPE
    dk = nope + MLA_ROPE
    if sample:
        ckv_ref, kpe_ref, a_ref, ape_ref = outs
    else:
        ckv_ref, kpe_ref, q_ref, k_ref, v_ref = outs

    def rope(x, rot):
        x3 = x.reshape(bb, tt, LANES) * cos_ref[...] + rot.reshape(bb, tt, LANES) * sin_ref[...]
        return x3.reshape(rows, LANES)

    h = _norm_mod(x_ref[...], g_ref[...], sh_ref[...], sc_ref[...]).reshape(rows, d).astype(BF16)
    down = _dot(h, wd_ref[...])
    cq = down[:, :MLA_Q_RANK]
    ckv = down[:, MLA_Q_RANK:MLA_Q_RANK + MLA_KV_RANK]
    o = MLA_Q_RANK + MLA_KV_RANK
    kpe = rope(down[:, o:o + LANES], down[:, o + LANES:o + 2 * LANES])
    cq = (cq * _rms_scale(jnp.sum(cq * cq, axis=-1, keepdims=True), MLA_Q_RANK) * gq_ref[...]).astype(BF16)
    ckv = ckv * _rms_scale(jnp.sum(ckv * ckv, axis=-1, keepdims=True), MLA_KV_RANK) * gkv_ref[...]
    ckv_ref[...] = ckv.reshape(bb, tt, MLA_KV_RANK)
    kpe_ref[...] = kpe[:, :MLA_ROPE].reshape(bb, tt, MLA_ROPE)
    ckv_b = ckv.astype(BF16)

    qa = _dot(cq, wuq_ref[...])
    for hh in range(nh):
        qn = qa[:, LANES * hh:LANES * (hh + 1)]
        qp = rope(qa[:, LANES * (nh + hh):LANES * (nh + hh + 1)],
                  qa[:, LANES * (2 * nh + hh):LANES * (2 * nh + hh + 1)])
        r = _rms_scale(jnp.sum(qn * qn, axis=-1, keepdims=True)
                       + jnp.sum(qp * qp, axis=-1, keepdims=True), dk)
        qn = qn * r * qn_ref[...]
        qp = qp * r * qp_ref[...]
        if sample:
            qk = (qn * kn_ref[...]).astype(BF16)
            a = _dot(qk, wuk_ref[hh])
            a_ref[:, :, 2 * LANES * hh:2 * LANES * (hh + 1)] = a.reshape(bb, tt, 2 * LANES)
            ape_ref[:, :, LANES * hh:LANES * (hh + 1)] = (qp * kp_ref[...]).reshape(bb, tt, LANES)
        else:
            q_ref[:, :, 2 * LANES * hh:2 * LANES * hh + LANES] = qn.reshape(bb, tt, LANES).astype(q_ref.dtype)
            q_ref[:, :, 2 * LANES * hh + LANES:2 * LANES * (hh + 1)] = qp.reshape(bb, tt, LANES).astype(q_ref.dtype)

    if not sample:
        kn_all = _dot(ckv_b, wuk_ref[...])
        pe_sq = jnp.sum(kpe * kpe, axis=-1, keepdims=True)
        for hh in range(nh):
            kn = kn_all[:, LANES * hh:LANES * (hh + 1)]
            r = _rms_scale(jnp.sum(kn * kn, axis=-1, keepdims=True) + pe_sq, dk)
            k_ref[:, :, 2 * LANES * hh:2 * LANES * hh + LANES] = (kn * r * kn_ref[...]).reshape(bb, tt, LANES).astype(k_ref.dtype)
            k_ref[:, :, 2 * LANES * hh + LANES:2 * LANES * (hh + 1)] = (kpe * r * kp_ref[...]).reshape(bb, tt, LANES).astype(k_ref.dtype)
        v_ref[...] = _dot(ckv_b, wuv_ref[...]).reshape(bb, tt, nh * LANES).astype(v_ref.dtype)


def _mla_weights(w_dqkv, w_uq, w_uk, w_uv, q_norm, k_norm):
    d = w_dqkv.shape[0]
    half = MLA_ROPE // 2
    nh = MLA_HEADS

    def rot_cols(w):
        return jnp.concatenate([-w[..., half:], w[..., :half]], axis=-1)

    def pad_lanes(w):
        return jnp.pad(w, [(0, 0)] * (w.ndim - 1) + [(0, LANES - w.shape[-1])])

    o = MLA_Q_RANK + MLA_KV_RANK
    w_pe = w_dqkv[:, o:]
    wd = jnp.concatenate([w_dqkv[:, :o], pad_lanes(w_pe), pad_lanes(rot_cols(w_pe))], axis=1).astype(BF16)
    uq_n = w_uq[:, :, :MLA_NOPE].reshape(MLA_Q_RANK, nh * MLA_NOPE)
    uq_p = w_uq[:, :, MLA_NOPE:]
    wuq = jnp.concatenate([uq_n, pad_lanes(uq_p).reshape(MLA_Q_RANK, nh * LANES),
                           pad_lanes(rot_cols(uq_p)).reshape(MLA_Q_RANK, nh * LANES)], axis=1).astype(BF16)

    def gain_pe(gn):
        pe = gn[MLA_NOPE:]
        return pad_lanes(jnp.concatenate([pe, pe])).reshape(1, LANES)

    scale = (MLA_NOPE + MLA_ROPE) ** -0.5
    gains = (q_norm[:MLA_NOPE].reshape(1, MLA_NOPE) * scale, gain_pe(q_norm) * scale,
             k_norm[:MLA_NOPE].reshape(1, MLA_NOPE), gain_pe(k_norm))
    wuk2 = w_uk.reshape(MLA_KV_RANK, nh * MLA_NOPE).astype(BF16)
    wuk_t = jnp.transpose(w_uk, (1, 2, 0)).astype(BF16)
    wuv2 = w_uv.reshape(MLA_KV_RANK, nh * MLA_NOPE).astype(BF16)
    return wd, wuq, wuk2, wuk_t, wuv2, gains


def _rope_tables(pos):
    half = MLA_ROPE // 2
    freqs = jnp.power(ROPE_THETA, -jnp.arange(half, dtype=F32) / half)
    ang = pos.astype(F32)[:, None] * freqs[None, :]
    cos = jnp.tile(jnp.cos(ang), (1, LANES // half))
    sin = jnp.tile(jnp.sin(ang), (1, LANES // half))
    return cos[None], sin[None]


def _mla_proj(x, g, shift, scale, pos, wts, g_q, g_kv, sample, prompt_tile):
    nb, r, d = x.shape
    wd, wuq, wuk2, wuk_t, wuv2, gains = wts
    bb, tt = _tok_tiles(x, prompt_tile)
    cos, sin = _rope_tables(pos)
    nh = MLA_HEADS
    wuk = wuk_t if sample else wuk2
    const = lambda a: pl.BlockSpec(a.shape, lambda b, t: (0,) * a.ndim)
    tok = lambda n: pl.BlockSpec((bb, tt, n), lambda b, t: (b, t, 0))
    mod = pl.BlockSpec((bb, 1, d), lambda b, t: (b, 0, 0))
    tab = pl.BlockSpec((1, tt, LANES), lambda b, t: (0, t, 0))
    gq = g_q.reshape(1, MLA_Q_RANK)
    gkv = g_kv.reshape(1, MLA_KV_RANK)
    g3 = g.reshape(1, 1, d)
    if sample:
        outs = [(MLA_KV_RANK, F32), (MLA_ROPE, F32), (nh * 2 * LANES, F32), (nh * LANES, F32)]
    else:
        outs = [(MLA_KV_RANK, F32), (MLA_ROPE, F32), (nh * 2 * LANES, BF16), (nh * 2 * LANES, BF16), (nh * LANES, BF16)]
    return pl.pallas_call(
        functools.partial(_mla_proj_body, sample=sample),
        grid=(nb // bb, r // tt),
        in_specs=[tok(d), const(g3), mod, mod, tab, tab, const(wd), const(gq), const(gkv),
                  const(wuq), const(wuk), const(wuv2)] + [const(a) for a in gains],
        out_specs=[tok(n) for n, _ in outs],
        out_shape=[jax.ShapeDtypeStruct((nb, r, n), dt) for n, dt in outs],
        compiler_params=_cparams(("parallel", "parallel")),
        name="mla_proj_sample" if sample else "mla_proj",
    )(x, g3, shift, scale, cos, sin, wd, gq, gkv, wuq, wuk, wuv2, *gains)


def _flash_body(q_ref, k_ref, v_ref, sl_ref, o_ref, qs_scr, m_scr, l_scr, acc_scr, *, n_stack, alibi, split_maps):
    tq = q_ref.shape[1]
    tk = k_ref.shape[1]
    dv = v_ref.shape[2]
    rows = n_stack * tq
    i = pl.program_id(2)
    j = pl.program_id(3)
    last = ((i + 1) * tq - 1) // tk

    @pl.when(j == 0)
    def _():
        m_scr[...] = jnp.full(m_scr.shape, NEG_INF, F32)
        l_scr[...] = jnp.zeros(l_scr.shape, F32)
        acc_scr[...] = jnp.zeros(acc_scr.shape, F32)
        q = q_ref[0]
        if split_maps:
            lane = lax.broadcasted_iota(jnp.int32, (tq, LANES), 1)
            c = 0
            for hh in range(q.shape[1] // LANES):
                qh = q[:, LANES * hh:LANES * (hh + 1)]
                for mp in range(2):
                    keep = (lane < DA_HEAD_DIM) if mp == 0 else (lane >= DA_HEAD_DIM)
                    qs_scr[c * tq:(c + 1) * tq, :] = jnp.where(keep, qh, 0.0).astype(BF16)
                    c += 1
        else:
            qs_scr[...] = q.astype(BF16)

    @pl.when(j <= last)
    def _():
        k = k_ref[0].astype(BF16)
        v = v_ref[0].astype(BF16)
        s = _dot_t(qs_scr[...], k)
        kpos = j * tk + lax.broadcasted_iota(jnp.int32, (rows, tk), 1)
        qrow = lax.broadcasted_iota(jnp.int32, (rows, tk), 0)
        if n_stack > 1:
            qrow = qrow % tq
        if alibi:
            s = s + sl_ref[0] * (kpos - i * tq).astype(F32)
        s = jnp.where(kpos <= qrow + i * tq, s, NEG_INF)
        m_old = m_scr[...]
        m_new = jnp.maximum(m_old, jnp.max(s, axis=-1, keepdims=True))
        corr = jnp.exp(m_old - m_new)
        p = jnp.exp(s - m_new)
        l_scr[...] = l_scr[...] * corr + jnp.sum(p, axis=-1, keepdims=True)
        acc_scr[...] = acc_scr[...] * corr + _dot(p.astype(BF16), v)
        m_scr[...] = m_new

    @pl.when(j == pl.num_programs(3) - 1)
    def _():
        o = acc_scr[...] / l_scr[...]
        for c in range(n_stack):
            o_ref[0, :, dv * c:dv * (c + 1)] = o[c * tq:(c + 1) * tq].astype(o_ref.dtype)


def _flash(q_arr, k_arr, v_arr, slopes, *, n_groups, n_stack, dq, dk, dv, q_col, k_col, v_col,
           tq, tk, alibi, split_maps, out_dtype):
    b, t, _ = q_arr.shape
    tq, tk = min(tq, t), min(tk, t)
    nq, nk = t // tq, t // tk
    rows = n_stack * tq

    def kv_idx(col):
        def f(bi, g, i, j):
            return (bi, jnp.minimum(j, ((i + 1) * tq - 1) // tk), col + g)
        return f

    body = functools.partial(_flash_body, n_stack=n_stack, alibi=alibi, split_maps=split_maps)
    return pl.pallas_call(
        body,
        grid=(b, n_groups, nq, nk),
        in_specs=[pl.BlockSpec((1, tq, dq), lambda bi, g, i, j: (bi, i, q_col + g)),
                  pl.BlockSpec((1, tk, dk), kv_idx(k_col)),
                  pl.BlockSpec((1, tk, dv), kv_idx(v_col)),
                  pl.BlockSpec((1, rows, 1), lambda bi, g, i, j: (g, 0, 0))],
        out_specs=pl.BlockSpec((1, tq, n_stack * dv), lambda bi, g, i, j: (bi, i, g)),
        out_shape=jax.ShapeDtypeStruct((b, t, n_groups * n_stack * dv), out_dtype),
        scratch_shapes=[pltpu.VMEM((rows, dk), BF16), pltpu.VMEM((rows, 1), F32),
                        pltpu.VMEM((rows, 1), F32), pltpu.VMEM((rows, dv), F32)],
        compiler_params=_cparams(("parallel", "parallel", "parallel", "arbitrary")),
        name="flash",
    )(q_arr, k_arr, v_arr, slopes)


def _band_body(*refs, tile, first_tile_flags):
    n_g = len(DL_CONFIGS)
    sl_ref = refs[0]
    grp = [refs[1 + 5 * g:1 + 5 * (g + 1)] for g in range(n_g)]
    o_ref = refs[1 + 5 * n_g]
    m_scr, l_scr, acc_scr = refs[2 + 5 * n_g:]
    i = pl.program_id(1)
    nb = N_BACK
    slope = sl_ref[0]
    a_i = lax.broadcasted_iota(jnp.int32, (nb, 2 * nb), 0)
    b_i = lax.broadcasted_iota(jnp.int32, (nb, 2 * nb), 1)
    steps = a_i - b_i + nb
    band = (steps >= 0) & (steps <= nb)
    band_first = band & (b_i >= jnp.where(i == 0, nb, 0))

    for g, (win, dil) in enumerate(DL_CONFIGS):
        q_ref, kc_ref, vc_ref, kp_ref, vp_ref = grp[g]
        n_sub = tile // dil
        bias = -slope[:, :1] * (steps * dil).astype(F32)
        for r in range(dil):
            for qb in range(n_sub // nb):
                sel = pl.ds(r + dil * nb * qb, nb, stride=dil) if dil > 1 else pl.ds(nb * qb, nb)
                q = q_ref[0, sel, :].astype(BF16)
                if qb > 0:
                    psel = pl.ds(r + dil * nb * (qb - 1), nb, stride=dil) if dil > 1 else pl.ds(nb * (qb - 1), nb)
                    k_prev, v_prev = kc_ref[0, psel, :], vc_ref[0, psel, :]
                else:
                    psel = pl.ds(r, nb, stride=dil) if dil > 1 else pl.ds(0, nb)
                    k_prev, v_prev = kp_ref[0, psel, :], vp_ref[0, psel, :]
                k = jnp.concatenate([k_prev, kc_ref[0, sel, :]], axis=0).astype(BF16)
                v = jnp.concatenate([v_prev, vc_ref[0, sel, :]], axis=0).astype(BF16)
                s = _dot_t(q, k) + bias
                s = jnp.where(band_first if qb == 0 else band, s, NEG_INF)
                m_blk = jnp.max(s, axis=-1, keepdims=True)
                if g == 0:
                    m_new = jnp.broadcast_to(m_blk, (nb, LANES))
                    p = jnp.exp(s - m_blk)
                    l_new = jnp.broadcast_to(jnp.sum(p, axis=-1, keepdims=True), (nb, LANES))
                    acc_new = _dot(p.astype(BF16), v)
                else:
                    m_old = m_scr[sel, :]
                    m_new = jnp.maximum(m_old, m_blk)
                    corr = jnp.exp(m_old - m_new)
                    p = jnp.exp(s - m_new[:, :1])
                    l_new = l_scr[sel, :] * corr + jnp.sum(p, axis=-1, keepdims=True)
                    acc_new = acc_scr[sel, :] * corr + _dot(p.astype(BF16), v)
                m_scr[sel, :] = m_new
                l_scr[sel, :] = l_new
                acc_scr[sel, :] = acc_new
    o_ref[0] = (acc_scr[...] / l_scr[...]).astype(o_ref.dtype)


def _band_attention(qkv, slopes_lane, tile, out_dtype):
    b, t, _ = qkv.shape
    nh = DL_HEADS
    tile = min(tile, t)
    in_specs = [pl.BlockSpec((1, 1, LANES), lambda bi, i, h: (h, 0, 0))]
    args = [slopes_lane]
    for g, (win, dil) in enumerate(DL_CONFIGS):
        halo = N_BACK * dil
        per = tile // halo
        col = 3 * nh * g
        in_specs += [
            pl.BlockSpec((1, tile, LANES), lambda bi, i, h, c=col: (bi, i, c + h)),
            pl.BlockSpec((1, tile, LANES), lambda bi, i, h, c=col: (bi, i, c + nh + h)),
            pl.BlockSpec((1, tile, LANES), lambda bi, i, h, c=col: (bi, i, c + 2 * nh + h)),
            pl.BlockSpec((1, halo, LANES), lambda bi, i, h, c=col, p=per: (bi, jnp.maximum(i * p - 1, 0), c + nh + h)),
            pl.BlockSpec((1, halo, LANES), lambda bi, i, h, c=col, p=per: (bi, jnp.maximum(i * p - 1, 0), c + 2 * nh + h)),
        ]
        args += [qkv] * 5
    return pl.pallas_call(
        functools.partial(_band_body, tile=tile, first_tile_flags=None),
        grid=(b, t // tile, nh),
        in_specs=in_specs,
        out_specs=pl.BlockSpec((1, tile, LANES), lambda bi, i, h: (bi, i, h)),
        out_shape=jax.ShapeDtypeStruct((b, t, nh * LANES), out_dtype),
        scratch_shapes=[pltpu.VMEM((tile, LANES), F32)] * 3,
        compiler_params=_cparams(("parallel", "parallel", "parallel")),
        name="band_attention",
    )(*args)


def _out_proj_body(*refs, mode, lam_init):
    if mode == "plain":
        a_ref, wo_ref, x_ref, gate_ref, o_ref = refs
    elif mode == "diff":
        a_ref, lq1, lk1, lq2, lk2, sub_ref, wo_ref, x_ref, gate_ref, o_ref = refs
    else:
        o0, o1, o2, e0, e1, e2, wo_ref, x_ref, gate_ref, o_ref = refs
    bb, tt, d = x_ref.shape
    rows = bb * tt
    if mode == "plain":
        a = a_ref[...].reshape(rows, a_ref.shape[2]).astype(BF16)
    elif mode == "diff":
        lam = (jnp.exp(jnp.sum(lq1[...] * lk1[...], axis=-1, keepdims=True))
               - jnp.exp(jnp.sum(lq2[...] * lk2[...], axis=-1, keepdims=True)) + lam_init)
        o = a_ref[...].reshape(rows, a_ref.shape[2]).astype(F32)
        parts = []
        for hh in range(DA_HEADS):
            dlt = o[:, 2 * LANES * hh:2 * LANES * hh + LANES] - lam * o[:, 2 * LANES * hh + LANES:2 * LANES * (hh + 1)]
            r = _rms_scale(jnp.sum(dlt * dlt, axis=-1, keepdims=True), LANES)
            parts.append(dlt * r * sub_ref[...] * (1.0 - lam_init))
        a = jnp.concatenate(parts, axis=-1).astype(BF16)
    else:
        e = [e0[...], e1[...], e2[...]]
        mx = jnp.maximum(jnp.maximum(e[0], e[1]), e[2])
        w = [jnp.exp(ei - mx) for ei in e]
        den = w[0] + w[1] + w[2]
        a = (w[0] * o0[...] + w[1] * o1[...] + w[2] * o2[...]) / den
        a = a.reshape(rows, d).astype(BF16)
    y = _dot(a, wo_ref[...])
    o_ref[...] = x_ref[...] + gate_ref[...] * y.reshape(bb, tt, d)


def _out_proj(acts, extras, wo_bf, x, gate, mode, lam_init, prompt_tile):
    nb, r, d = x.shape
    bb, tt = _tok_tiles(x, prompt_tile)
    tok = lambda n: pl.BlockSpec((bb, tt, n), lambda b, t: (b, t, 0))
    const = lambda a: pl.BlockSpec(a.shape, lambda b, t: (0,) * a.ndim)
    in_specs = [tok(a.shape[2]) for a in acts] + [const(e) for e in extras] + [
        const(wo_bf), tok(d), pl.BlockSpec((bb, 1, d), lambda b, t: (b, 0, 0))]
    return pl.pallas_call(
        functools.partial(_out_proj_body, mode=mode, lam_init=lam_init),
        grid=(nb // bb, r // tt),
        in_specs=in_specs,
        out_specs=tok(d),
        out_shape=jax.ShapeDtypeStruct((nb, r, d), F32),
        compiler_params=_cparams(("parallel", "parallel")),
        name="out_proj_" + mode,
    )(*acts, *extras, wo_bf, x, gate)


def _ffn_body(x_ref, g_ref, sh_ref, sc_ref, gate_ref, w1_ref, w3_ref, w2_ref, o_ref, h_scr, acc_scr):
    bb, tt, d = x_ref.shape
    rows = bb * tt
    j = pl.program_id(2)

    @pl.when(j == 0)
    def _():
        h = _norm_mod(x_ref[...], g_ref[...], sh_ref[...], sc_ref[...])
        h_scr[...] = h.reshape(rows, d).astype(BF16)
        acc_scr[...] = jnp.zeros(acc_scr.shape, F32)

    h = h_scr[...]
    a = _dot(h, w1_ref[...])
    b = _dot(h, w3_ref[...])
    u = (a * jax.nn.sigmoid(a) * b).astype(BF16)
    acc_scr[...] += _dot(u, w2_ref[...])

    @pl.when(j == pl.num_programs(2) - 1)
    def _():
        o_ref[...] = x_ref[...] + gate_ref[...] * acc_scr[...].reshape(bb, tt, d)


def _ffn(x, g, shift, scale, gate, w1, w3, w2, th, prompt_tile):
    nb, r, d = x.shape
    hid = w1.shape[1]
    bb, tt = _tok_tiles(x, prompt_tile)
    mod = pl.BlockSpec((bb, 1, d), lambda b, t, j: (b, 0, 0))
    return pl.pallas_call(
        _ffn_body,
        grid=(nb // bb, r // tt, hid // th),
        in_specs=[pl.BlockSpec((bb, tt, d), lambda b, t, j: (b, t, 0)),
                  pl.BlockSpec((1, 1, d), lambda b, t, j: (0, 0, 0)),
                  mod, mod, mod,
                  pl.BlockSpec((d, th), lambda b, t, j: (0, j)),
                  pl.BlockSpec((d, th), lambda b, t, j: (0, j)),
                  pl.BlockSpec((th, d), lambda b, t, j: (j, 0))],
        out_specs=pl.BlockSpec((bb, tt, d), lambda b, t, j: (b, t, 0)),
        out_shape=jax.ShapeDtypeStruct((nb, r, d), F32),
        scratch_shapes=[pltpu.VMEM((bb * tt, d), BF16), pltpu.VMEM((bb * tt, d), F32)],
        compiler_params=_cparams(("parallel", "parallel", "arbitrary")),
        name="ffn",
    )(x, g.reshape(1, 1, d), shift, scale, gate, w1, w3, w2)


def _softmax_update(s, m_scr, l_scr):
    m_old = m_scr[...]
    m_new = jnp.maximum(m_old, jnp.max(s, axis=-1, keepdims=True))
    corr = jnp.exp(m_old - m_new)
    p = jnp.exp(s - m_new)
    l_scr[...] = l_scr[...] * corr + jnp.sum(p, axis=-1, keepdims=True)
    m_scr[...] = m_new
    return p, corr


def _mla_decode_body(pt_ref, a_ref, ape_ref, ckv_ref, kpe_ref, wuk_ref, wuv_ref, *rest, pages, sub):
    lat_refs = rest[:pages]
    pe_refs = rest[pages:2 * pages]
    o_ref = rest[2 * pages]
    aq_scr, apq_scr, m_scr, l_scr, acc_scr, latn_scr, pen_scr = rest[2 * pages + 1:]
    nh = MLA_HEADS
    s_tok = a_ref.shape[1]
    nrow = nh * s_tok
    c = pl.program_id(1)
    dk = MLA_NOPE + MLA_ROPE

    @pl.when(c == 0)
    def _():
        m_scr[...] = jnp.full(m_scr.shape, NEG_INF, F32)
        l_scr[...] = jnp.zeros(l_scr.shape, F32)
        acc_scr[...] = jnp.zeros(acc_scr.shape, F32)
        for hh in range(nh):
            aq_scr[s_tok * hh:s_tok * (hh + 1), :] = a_ref[0, :, 2 * LANES * hh:2 * LANES * (hh + 1)].astype(BF16)
            apq_scr[s_tok * hh:s_tok * (hh + 1), :] = ape_ref[0, :, LANES * hh:LANES * hh + MLA_ROPE].astype(BF16)

    ones = jnp.ones((s_tok, MLA_ROPE), BF16)

    def absorb(lat, pe, mask):
        nk = lat.shape[0]
        latb = lat.astype(BF16)
        peb = pe.astype(BF16)
        kt = _dot_t(wuk_ref[...], latb)
        pe_sq = _dot_t(ones, (pe * pe).astype(BF16))
        ssq = []
        for hh in range(nh):
            kh = kt[MLA_NOPE * hh:MLA_NOPE * (hh + 1), :]
            ssq.append(jnp.broadcast_to(jnp.sum(kh * kh, axis=0, keepdims=True), (s_tok, nk)) + pe_sq)
        r = _rms_scale(jnp.concatenate(ssq, axis=0), dk)
        s = (_dot_t(aq_scr[...], latb) + _dot_t(apq_scr[...], peb)) * r
        if mask is not None:
            s = jnp.where(mask, s, NEG_INF)
        p, corr = _softmax_update(s, m_scr, l_scr)
        acc_scr[...] = acc_scr[...] * corr + _dot(p.astype(BF16), latb)

    per = sub // lat_refs[0].shape[1]
    for u in range(pages // per):
        lat = jnp.concatenate([lat_refs[u * per + v][0] for v in range(per)], axis=0)
        pe = jnp.concatenate([pe_refs[u * per + v][0] for v in range(per)], axis=0)
        absorb(lat, pe, None)

    @pl.when(c == pl.num_programs(1) - 1)
    def _():
        latn_scr[...] = jnp.zeros(latn_scr.shape, F32)
        pen_scr[...] = jnp.zeros(pen_scr.shape, F32)
        latn_scr[0:s_tok, :] = ckv_ref[0]
        pen_scr[0:s_tok, :] = kpe_ref[0]
        qi = lax.broadcasted_iota(jnp.int32, (nrow, LANES), 0) % s_tok
        kj = lax.broadcasted_iota(jnp.int32, (nrow, LANES), 1)
        absorb(latn_scr[...], pen_scr[...], kj <= qi)
        o_lat = (acc_scr[...] / l_scr[...]).astype(BF16)
        for hh in range(nh):
            o_ref[0, :, LANES * hh:LANES * (hh + 1)] = _dot(
                o_lat[s_tok * hh:s_tok * (hh + 1), :], wuv_ref[:, LANES * hh:LANES * (hh + 1)])


def _mla_decode(a, ape, ckv_s, kpe_s, cache_lat, cache_kpe, page_table, wuk_rows, wuv2, pages):
    db, s_tok, _ = a.shape
    n_pages = page_table.shape[1]
    page = cache_lat.shape[1]
    pages = min(pages, n_pages)
    nh = MLA_HEADS
    pt = page_table.reshape(-1)

    def page_spec(width, jj):
        return pl.BlockSpec((1, page, width), lambda b, c, pt_ref: (pt_ref[b * n_pages + c * pages + jj], 0, 0))

    seq = lambda n: pl.BlockSpec((1, s_tok, n), lambda b, c, pt_ref: (b, 0, 0))
    const = lambda arr: pl.BlockSpec(arr.shape, lambda b, c, pt_ref: (0,) * arr.ndim)
    grid_spec = pltpu.PrefetchScalarGridSpec(
        num_scalar_prefetch=1,
        grid=(db, n_pages // pages),
        in_specs=[seq(a.shape[2]), seq(ape.shape[2]), seq(MLA_KV_RANK), seq(MLA_ROPE), const(wuk_rows), const(wuv2)]
                 + [page_spec(MLA_KV_RANK, jj) for jj in range(pages)]
                 + [page_spec(MLA_ROPE, jj) for jj in range(pages)],
        out_specs=pl.BlockSpec((1, s_tok, nh * LANES), lambda b, c, pt_ref: (b, 0, 0)),
        scratch_shapes=[pltpu.VMEM((nh * s_tok, MLA_KV_RANK), BF16), pltpu.VMEM((nh * s_tok, MLA_ROPE), BF16),
                        pltpu.VMEM((nh * s_tok, 1), F32), pltpu.VMEM((nh * s_tok, 1), F32),
                        pltpu.VMEM((nh * s_tok, MLA_KV_RANK), F32),
                        pltpu.VMEM((LANES, MLA_KV_RANK), F32), pltpu.VMEM((LANES, MLA_ROPE), F32)],
    )
    return pl.pallas_call(
        functools.partial(_mla_decode_body, pages=pages, sub=2 * page),
        grid_spec=grid_spec,
        out_shape=jax.ShapeDtypeStruct((db, s_tok, nh * LANES), F32),
        compiler_params=_cparams(("parallel", "arbitrary")),
        name="mla_decode",
    )(pt, a, ape, ckv_s, kpe_s, wuk_rows, wuv2, *([cache_lat] * pages), *([cache_kpe] * pages))


def _diff_decode_body(pt_ref, qkv_ref, sl_ref, *rest, pages, sub, past):
    k_refs = rest[:pages]
    v_refs = rest[pages:2 * pages]
    o_ref = rest[2 * pages]
    qz_scr, m_scr, l_scr, acc_scr, kn_scr, vn_scr = rest[2 * pages + 1:]
    s_tok = qkv_ref.shape[1]
    nkv = DA_KV_HEADS
    per_kv = (DA_HEADS // nkv) * 2 * s_tok
    nrow = nkv * per_kv
    page = k_refs[0].shape[1]
    c = pl.program_id(1)
    nq = DA_HEADS * 2 * DA_HEAD_DIM

    @pl.when(c == 0)
    def _():
        m_scr[...] = jnp.full(m_scr.shape, NEG_INF, F32)
        l_scr[...] = jnp.zeros(l_scr.shape, F32)
        acc_scr[...] = jnp.zeros(acc_scr.shape, F32)
        lane = lax.broadcasted_iota(jnp.int32, (s_tok, LANES), 1)
        row = 0
        for hh in range(DA_HEADS):
            qh = qkv_ref[0, :, LANES * hh:LANES * (hh + 1)]
            for mp in range(2):
                keep = (lane < DA_HEAD_DIM) if mp == 0 else (lane >= DA_HEAD_DIM)
                qz_scr[row:row + s_tok, :] = jnp.where(keep, qh, 0.0).astype(BF16)
                row += s_tok

    def absorb(k, v, kpos0, mask):
        nk = k.shape[0]
        kb = k.astype(BF16)
        vb = v.astype(BF16)
        s = jnp.concatenate(
            [_dot_t(qz_scr[per_kv * kv:per_kv * (kv + 1), :], kb[:, LANES * kv:LANES * (kv + 1)]) for kv in range(nkv)],
            axis=0)
        rel = kpos0 + lax.broadcasted_iota(jnp.int32, (nrow, nk), 1)
        s = s + sl_ref[...] * rel.astype(F32)
        if mask is not None:
            s = jnp.where(mask, s, NEG_INF)
        p, corr = _softmax_update(s, m_scr, l_scr)
        pb = p.astype(BF16)
        pv = jnp.concatenate(
            [_dot(pb[per_kv * kv:per_kv * (kv + 1), :], vb[:, LANES * kv:LANES * (kv + 1)]) for kv in range(nkv)],
            axis=0)
        acc_scr[...] = acc_scr[...] * corr + pv

    per = sub // page
    for u in range(pages // per):
        k = jnp.concatenate([k_refs[u * per + w][0] for w in range(per)], axis=0)
        v = jnp.concatenate([v_refs[u * per + w][0] for w in range(per)], axis=0)
        absorb(k, v, (c * pages + u * per) * page - past, None)

    @pl.when(c == pl.num_programs(1) - 1)
    def _():
        kn_scr[...] = jnp.zeros(kn_scr.shape, F32)
        vn_scr[...] = jnp.zeros(vn_scr.shape, F32)
        kn_scr[0:s_tok, :] = qkv_ref[0, :, nq:nq + nkv * LANES]
        vn_scr[0:s_tok, :] = qkv_ref[0, :, nq + nkv * LANES:nq + 2 * nkv * LANES]
        qi = lax.broadcasted_iota(jnp.int32, (nrow, LANES), 0) % s_tok
        kj = lax.broadcasted_iota(jnp.int32, (nrow, LANES), 1)
        absorb(kn_scr[...], vn_scr[...], 0, kj <= qi)
        o_ref[0] = acc_scr[...] / l_scr[...]


def _diff_decode(qkv_s, cache_k, cache_v, page_table, slopes_rows, pages):
    db, s_tok, _ = qkv_s.shape
    n_pages = page_table.shape[1]
    page = cache_k.shape[1]
    pages = min(pages, n_pages)
    width = DA_KV_HEADS * LANES
    nrow = DA_HEADS * 2 * s_tok
    pt = page_table.reshape(-1)

    def page_spec(jj):
        return pl.BlockSpec((1, page, width), lambda b, c, pt_ref: (pt_ref[b * n_pages + c * pages + jj], 0, 0))

    grid_spec = pltpu.PrefetchScalarGridSpec(
        num_scalar_prefetch=1,
        grid=(db, n_pages // pages),
        in_specs=[pl.BlockSpec((1, s_tok, qkv_s.shape[2]), lambda b, c, pt_ref: (b, 0, 0)),
                  pl.BlockSpec((nrow, 1), lambda b, c, pt_ref: (0, 0))]
                 + [page_spec(jj) for jj in range(pages)] * 2,
        out_specs=pl.BlockSpec((1, nrow, LANES), lambda b, c, pt_ref: (b, 0, 0)),
        scratch_shapes=[pltpu.VMEM((nrow, LANES), BF16), pltpu.VMEM((nrow, 1), F32), pltpu.VMEM((nrow, 1), F32),
                        pltpu.VMEM((nrow, LANES), F32),
                        pltpu.VMEM((LANES, width), F32), pltpu.VMEM((LANES, width), F32)],
    )
    return pl.pallas_call(
        functools.partial(_diff_decode_body, pages=pages, sub=2 * page, past=n_pages * page),
        grid_spec=grid_spec,
        out_shape=jax.ShapeDtypeStruct((db, nrow, LANES), F32),
        compiler_params=_cparams(("parallel", "arbitrary")),
        name="diff_decode",
    )(pt, qkv_s, slopes_rows, *([cache_k.reshape(-1, page, width)] * pages), *([cache_v.reshape(-1, page, width)] * pages))


def _dil_decode_body(qkv_ref, sl_ref, buf_ref, o_ref, lse_ref, kn_scr, vn_scr, *, dil, n_res):
    s_tok = qkv_ref.shape[1]
    nh = DL_HEADS
    hd = nh * DL_HEAD_DIM
    nrow = nh * s_tok
    nb = buf_ref.shape[1]
    q = qkv_ref[0, :, :hd]
    row_h = lax.broadcasted_iota(jnp.int32, (nrow, hd), 0) // s_tok
    col_h = lax.broadcasted_iota(jnp.int32, (nrow, hd), 1) // DL_HEAD_DIM
    qbd = jnp.where(row_h == col_h, jnp.concatenate([q] * nh, axis=0), 0.0).astype(BF16)
    kn_scr[...] = jnp.zeros(kn_scr.shape, F32)
    vn_scr[...] = jnp.zeros(vn_scr.shape, F32)
    kn_scr[0:s_tok, :] = qkv_ref[0, :, hd:2 * hd]
    vn_scr[0:s_tok, :] = qkv_ref[0, :, 2 * hd:3 * hd]

    qi = lax.broadcasted_iota(jnp.int32, (nrow, nb), 0) % s_tok
    kj = lax.broadcasted_iota(jnp.int32, (nrow, nb), 1)
    q_res, q_t = qi % dil, qi // dil
    slope = sl_ref[...]
    blocks = []
    for r in range(n_res):
        k = jnp.concatenate([buf_ref[0, :, r, 0, hh, :] for hh in range(nh)], axis=-1).astype(BF16)
        v = jnp.concatenate([buf_ref[0, :, r, 1, hh, :] for hh in range(nh)], axis=-1).astype(BF16)
        steps = nb + q_t - kj
        s = _dot_t(qbd, k) - slope * (steps * dil).astype(F32)
        blocks.append((jnp.where((q_res == r) & (kj >= q_t), s, NEG_INF), v))
    s = _dot_t(qbd, kn_scr[...].astype(BF16)) - slope * (qi - kj).astype(F32)
    ok = (kj < s_tok) & (kj <= qi) & (kj % dil == q_res)
    blocks.append((jnp.where(ok, s, NEG_INF), vn_scr[...].astype(BF16)))

    m = blocks[0][0].max(axis=-1, keepdims=True)
    for s, _ in blocks[1:]:
        m = jnp.maximum(m, s.max(axis=-1, keepdims=True))
    l = jnp.zeros((nrow, 1), F32)
    acc = jnp.zeros((nrow, hd), F32)
    for s, v in blocks:
        p = jnp.exp(s - m)
        l = l + jnp.sum(p, axis=-1, keepdims=True)
        acc = acc + _dot(p.astype(BF16), v)
    out = acc / l
    lse = m + jnp.log(l)
    for hh in range(nh):
        rs = slice(s_tok * hh, s_tok * (hh + 1))
        cs = slice(DL_HEAD_DIM * hh, DL_HEAD_DIM * (hh + 1))
        o_ref[0, :, cs] = out[rs, cs]
        lse_ref[0, :, cs] = jnp.broadcast_to(lse[rs], (s_tok, DL_HEAD_DIM))


def _dil_decode(qkv_s, buf, slopes_rows, g, dil):
    db, s_tok, _ = qkv_s.shape
    nh = DL_HEADS
    hd = nh * DL_HEAD_DIM
    wb = buf.shape[1]
    assert wb == N_BACK * dil, "window buffer must hold the whole window"
    n_res = min(dil, s_tok)
    bufv = buf.reshape(db, wb // dil, dil, 2, nh, DL_HEAD_DIM)
    out = jax.ShapeDtypeStruct((db, s_tok, hd), F32)
    return pl.pallas_call(
        functools.partial(_dil_decode_body, dil=dil, n_res=n_res),
        grid=(db,),
        in_specs=[pl.BlockSpec((1, s_tok, 3 * hd), lambda b: (b, 0, g)),
                  pl.BlockSpec((nh * s_tok, 1), lambda b: (0, 0)),
                  pl.BlockSpec((1, wb // dil, n_res, 2, nh, DL_HEAD_DIM), lambda b: (b, 0, 0, 0, 0, 0))],
        out_specs=[pl.BlockSpec((1, s_tok, hd), lambda b: (b, 0, 0))] * 2,
        out_shape=[out, out],
        scratch_shapes=[pltpu.VMEM((LANES, hd), F32), pltpu.VMEM((LANES, hd), F32)],
        compiler_params=_cparams(("parallel",)),
        name="dil_decode",
    )(qkv_s, slopes_rows, bufv)


PROMPT_TILE = 512
FFN_TILE = 1024
FFN_HIDDEN_TILE = 256
DECODE_PAGES = 8


def _alibi_slopes(n):
    return np.array([2.0 ** (-8.0 * (i + 1) / n) for i in range(n)], dtype=np.float32)


def _mla_layer(hp_args, hs_args, caches, page_table, params):
    w_dqkv, g_q, g_kv, w_uq, w_uk, w_uv, q_norm, k_norm, w_o = params
    xp, gmix, shp, scp = hp_args
    xs, _, shs, scs = hs_args
    cache_lat, cache_kpe = caches
    wts = _mla_weights(w_dqkv, w_uq, w_uk, w_uv, q_norm, k_norm)
    b, t, _ = xp.shape
    db, s_tok, _ = xs.shape
    past = page_table.shape[1] * cache_lat.shape[1]
    ckv_p, kpe_p, q, k, v = _mla_proj(xp, gmix, shp, scp, jnp.arange(t), wts, g_q, g_kv, False, PROMPT_TILE)
    nh = MLA_HEADS
    o_p = _flash(q, k, v, jnp.zeros((nh, min(1024, t), 1), F32), n_groups=nh, n_stack=1,
                 dq=2 * LANES, dk=2 * LANES, dv=LANES, q_col=0, k_col=0, v_col=0,
                 tq=1024, tk=512, alibi=False, split_maps=False, out_dtype=BF16)
    ckv_s, kpe_s, a, ape = _mla_proj(xs, gmix, shs, scs, past + jnp.arange(s_tok), wts, g_q, g_kv, True, PROMPT_TILE)
    wuk_rows = wts[3].reshape(nh * MLA_NOPE, MLA_KV_RANK)
    o_s = _mla_decode(a, ape, ckv_s, kpe_s, cache_lat, cache_kpe, page_table, wuk_rows, wts[4], DECODE_PAGES)
    return (o_p,), (o_s,), "plain", (), w_o, (ckv_p, ckv_s, kpe_p, kpe_s)


def _diff_layer(hp_args, hs_args, caches, page_table, layer_idx, params):
    w_qkv, q_norm, k_norm, lq1, lk1, lq2, lk2, subln, w_o = params
    xp, gmix, shp, scp = hp_args
    xs, _, shs, scs = hs_args
    cache_k, cache_v = caches
    b, t, _ = xp.shape
    db, s_tok, _ = xs.shape
    scale = DA_HEAD_DIM ** -0.5
    tn = 512
    rep = tn // DA_HEAD_DIM
    gains = jnp.stack([jnp.tile(q_norm, rep) * scale, jnp.tile(k_norm, rep), jnp.ones((tn,), F32)]).reshape(3, 1, tn)
    nq_tiles = DA_HEADS * 2 * DA_HEAD_DIM // tn
    nk_tiles = DA_KV_HEADS * 2 * DA_HEAD_DIM // tn
    gain_idx = lambda j: jnp.maximum(j - (nq_tiles - 1), 0)
    kind = lambda j: j >= nq_tiles + nk_tiles
    wq = w_qkv.astype(BF16)
    qkv_p = _group_proj(xp, gmix, shp, scp, wq, gains, gain_idx, kind, DA_HEAD_DIM, tn, PROMPT_TILE)
    qkv_s = _group_proj(xs, gmix, shs, scs, wq, gains, gain_idx, kind, DA_HEAD_DIM, tn, PROMPT_TILE)
    slopes = _alibi_slopes(DA_HEADS)
    tq = min(256, t)
    per_kv = DA_HEADS // DA_KV_HEADS
    sl_rows = np.repeat(slopes.reshape(DA_KV_HEADS, per_kv), 2 * tq, axis=1).reshape(DA_KV_HEADS, per_kv * 2 * tq, 1)
    kc = DA_HEADS * 2 * DA_HEAD_DIM // LANES
    o_p = _flash(qkv_p, qkv_p, qkv_p, jnp.asarray(sl_rows), n_groups=DA_KV_HEADS, n_stack=per_kv * 2,
                 dq=per_kv * LANES, dk=LANES, dv=LANES, q_col=0, k_col=kc, v_col=kc + DA_KV_HEADS,
                 tq=tq, tk=512, alibi=True, split_maps=True, out_dtype=BF16)
    sl_dec = jnp.asarray(np.repeat(slopes, 2 * s_tok).reshape(DA_HEADS * 2 * s_tok, 1))
    o_dec = _diff_decode(qkv_s, cache_k, cache_v, page_table, sl_dec, DECODE_PAGES)
    o_s = jnp.transpose(o_dec.reshape(db, DA_HEADS * 2, s_tok, LANES), (0, 2, 1, 3)).reshape(db, s_tok, DA_HEADS * 2 * LANES)
    nq = DA_HEADS * 2 * DA_HEAD_DIM
    nk = DA_KV_HEADS * 2 * DA_HEAD_DIM
    k_p = qkv_p[..., nq:nq + nk].reshape(b, t, DA_KV_HEADS, 2, DA_HEAD_DIM)
    k_s = qkv_s[..., nq:nq + nk].reshape(db, s_tok, DA_KV_HEADS, 2, DA_HEAD_DIM)
    v_p = qkv_p[..., nq + nk:].reshape(b, t, DA_KV_HEADS, 2 * DA_HEAD_DIM)
    v_s = qkv_s[..., nq + nk:].reshape(db, s_tok, DA_KV_HEADS, 2 * DA_HEAD_DIM)
    extras = tuple(a.reshape(1, -1) for a in (lq1, lk1, lq2, lk2, subln))
    return (o_p,), (o_s,), "diff", extras, w_o, (k_p, k_s, v_p, v_s)


def _dilated_layer(hp_args, hs_args, caches, params):
    w_qkv, q_norm, k_norm, w_o = params
    xp, gmix, shp, scp = hp_args
    xs, _, shs, scs = hs_args
    b, t, _ = xp.shape
    db, s_tok, _ = xs.shape
    nh, hd = DL_HEADS, DL_HEAD_DIM
    scale = hd ** -0.5
    tn = nh * hd
    gains = jnp.stack([jnp.tile(q_norm, nh) * scale, jnp.tile(k_norm, nh), jnp.ones((tn,), F32)]).reshape(3, 1, tn)
    wq = w_qkv.astype(BF16)
    gain_idx = lambda j: j % 3
    kind = lambda j: j % 3 == 2
    qkv_p = _group_proj(xp, gmix, shp, scp, wq, gains, gain_idx, kind, hd, tn, PROMPT_TILE)
    qkv_s = _group_proj(xs, gmix, shs, scs, wq, gains, gain_idx, kind, hd, tn, PROMPT_TILE)
    slopes = _alibi_slopes(nh)
    sl_lane = jnp.asarray(np.repeat(slopes, LANES).reshape(nh, 1, LANES))
    o_p = _band_attention(qkv_p, sl_lane, 2048, BF16)
    sl_rows = jnp.asarray(np.repeat(slopes, s_tok).reshape(nh * s_tok, 1))
    outs, lses, st = [], [], []
    qkv_p6 = qkv_p.reshape(b, t, len(DL_CONFIGS), 3, nh, hd)
    qkv_s6 = qkv_s.reshape(db, s_tok, len(DL_CONFIGS), 3, nh, hd)
    for g, (win, dil) in enumerate(DL_CONFIGS):
        o_g, lse_g = _dil_decode(qkv_s, caches[g], sl_rows, g, dil)
        outs.append(o_g)
        lses.append(lse_g)
        wb = min(win, t)
        st.append(qkv_p6[:, t - wb:, g, 1:3])
        st.append(jnp.concatenate([caches[g][:, s_tok:], qkv_s6[:, :, g, 1:3]], axis=1))
    return (o_p,), tuple(outs) + tuple(lses), ("plain", "dil"), (), w_o, tuple(st)


def kernel(x_prompt, x_sample, cache_l0_latent, cache_l0_kpe, cache_l1_k, cache_l1_v, cache_l2_kv_w128, cache_l2_kv_w512, cache_l2_kv_w2048, cache_l3_latent, cache_l3_kpe, page_table, c_prompt, c_sample, l0_ada_w, l0_ada_b, l0_norm_mix, l0_w_dqkv, l0_g_q, l0_g_kv, l0_w_uq, l0_w_uk, l0_w_uv, l0_q_norm, l0_k_norm, l0_w_o, l0_norm_ffn, l0_ffn_w1, l0_ffn_w3, l0_ffn_w2, l1_ada_w, l1_ada_b, l1_norm_mix, l1_w_qkv, l1_q_norm, l1_k_norm, l1_lambda_q1, l1_lambda_k1, l1_lambda_q2, l1_lambda_k2, l1_subln, l1_w_o, l1_norm_ffn, l1_ffn_w1, l1_ffn_w3, l1_ffn_w2, l2_ada_w, l2_ada_b, l2_norm_mix, l2_w_qkv, l2_q_norm, l2_k_norm, l2_w_o, l2_norm_ffn, l2_ffn_w1, l2_ffn_w3, l2_ffn_w2, l3_ada_w, l3_ada_b, l3_norm_mix, l3_w_dqkv, l3_g_q, l3_g_kv, l3_w_uq, l3_w_uk, l3_w_uv, l3_q_norm, l3_k_norm, l3_w_o, l3_norm_ffn, l3_ffn_w1, l3_ffn_w3, l3_ffn_w2):
    block = [
        (l0_ada_w, l0_ada_b, l0_norm_mix, l0_norm_ffn, l0_ffn_w1, l0_ffn_w3, l0_ffn_w2),
        (l1_ada_w, l1_ada_b, l1_norm_mix, l1_norm_ffn, l1_ffn_w1, l1_ffn_w3, l1_ffn_w2),
        (l2_ada_w, l2_ada_b, l2_norm_mix, l2_norm_ffn, l2_ffn_w1, l2_ffn_w3, l2_ffn_w2),
        (l3_ada_w, l3_ada_b, l3_norm_mix, l3_norm_ffn, l3_ffn_w1, l3_ffn_w3, l3_ffn_w2),
    ]
    mixer = [
        (l0_w_dqkv, l0_g_q, l0_g_kv, l0_w_uq, l0_w_uk, l0_w_uv, l0_q_norm, l0_k_norm, l0_w_o),
        (l1_w_qkv, l1_q_norm, l1_k_norm, l1_lambda_q1, l1_lambda_k1, l1_lambda_q2, l1_lambda_k2, l1_subln, l1_w_o),
        (l2_w_qkv, l2_q_norm, l2_k_norm, l2_w_o),
        (l3_w_dqkv, l3_g_q, l3_g_kv, l3_w_uq, l3_w_uk, l3_w_uv, l3_q_norm, l3_k_norm, l3_w_o),
    ]
    caches = [
        (cache_l0_latent, cache_l0_kpe),
        (cache_l1_k, cache_l1_v),
        (cache_l2_kv_w128, cache_l2_kv_w512, cache_l2_kv_w2048),
        (cache_l3_latent, cache_l3_kpe),
    ]
    b, t, d = x_prompt.shape
    db, s_tok, _ = x_sample.shape
    rows = b + db
    rows_pad = -(-rows // 8) * 8
    c_all = jnp.pad(jnp.concatenate([c_prompt, c_sample], axis=0), ((0, rows_pad - rows), (0, 0)))
    xp, xs = x_prompt, x_sample
    states = []
    for i in range(len(block)):
        ada_w, ada_b, g_mix, g_ffn, w1, w3, w2 = block[i]
        mod = _adaln(c_all, ada_w, ada_b)
        mod_p = mod[:, :b].reshape(6, b, 1, d)
        mod_s = mod[:, b:rows].reshape(6, db, 1, d)
        hp_args = (xp, g_mix, mod_p[0], mod_p[1])
        hs_args = (xs, g_mix, mod_s[0], mod_s[1])
        kind = i % 3
        lam_init = 0.0
        if kind == 0:
            a_p, a_s, mode, extras, w_o, st = _mla_layer(hp_args, hs_args, caches[i], page_table, mixer[i])
        elif kind == 1:
            lam_init = 0.8 - 0.6 * math.exp(-0.3 * i)
            a_p, a_s, mode, extras, w_o, st = _diff_layer(hp_args, hs_args, caches[i], page_table, i, mixer[i])
        else:
            a_p, a_s, mode, extras, w_o, st = _dilated_layer(hp_args, hs_args, caches[i], mixer[i])
        mode_p, mode_s = mode if isinstance(mode, tuple) else (mode, mode)
        wo_bf = w_o.astype(BF16)
        xp = _out_proj(a_p, extras, wo_bf, xp, mod_p[2], mode_p, lam_init, PROMPT_TILE)
        xs = _out_proj(a_s, extras, wo_bf, xs, mod_s[2], mode_s, lam_init, PROMPT_TILE)
        w1b, w3b, w2b = w1.astype(BF16), w3.astype(BF16), w2.astype(BF16)
        xp = _ffn(xp, g_ffn, mod_p[3], mod_p[4], mod_p[5], w1b, w3b, w2b, FFN_HIDDEN_TILE, FFN_TILE)
        xs = _ffn(xs, g_ffn, mod_s[3], mod_s[4], mod_s[5], w1b, w3b, w2b, FFN_HIDDEN_TILE, FFN_TILE)
        states.append(st)
    l0, l1, l2, l3 = states
    return (xp, xs, l0[0], l0[1], l0[2], l0[3], l1[0], l1[1], l1[2], l1[3],
            l2[0], l2[1], l2[2], l2[3], l2[4], l2[5], l3[0], l3[1], l3[2], l3[3])
```

```python
import functools
import math

import numpy as np
import jax
import jax.numpy as jnp
from jax import lax
from jax.experimental import pallas as pl
from jax.experimental.pallas import tpu as pltpu

F32 = jnp.float32
BF16 = jnp.bfloat16

RMS_EPS = 1e-6
NEG_INF = -1e30
ROPE_THETA = 10000.0

MLA_HEADS = 8
MLA_NOPE = 128
MLA_ROPE = 64
MLA_Q_RANK = 384
MLA_KV_RANK = 256
DA_HEAD_DIM = 64
DA_HEADS = 8
DA_KV_HEADS = 4
DL_CONFIGS = ((128, 1), (512, 4), (2048, 16))
DL_HEADS = 8
DL_HEAD_DIM = 128
N_BACK = 128

LANES = 128
VMEM_LIMIT = 48 * 1024 * 1024


def _cparams(sem):
    return pltpu.CompilerParams(dimension_semantics=sem, vmem_limit_bytes=VMEM_LIMIT)


def _dot(a, b):
    return jnp.dot(a, b, preferred_element_type=F32)


def _dot_t(a, b):
    return lax.dot_general(a, b, (((1,), (1,)), ((), ())), preferred_element_type=F32)


def _norm_mod(x, g, shift, scale):
    ms = jnp.mean(x * x, axis=-1, keepdims=True)
    y = x * lax.rsqrt(ms + RMS_EPS) * g
    return y * (1.0 + scale) + shift


def _rms_scale(sumsq, n):
    return lax.rsqrt(sumsq * (1.0 / n) + RMS_EPS)


def _tok_tiles(x, prompt_tile):
    nb, r, _ = x.shape
    if r >= prompt_tile:
        return 1, prompt_tile
    return min(nb, 128), r


def _adaln_body(c_ref, w_ref, b_ref, o_ref):
    c = c_ref[...]
    a = (c * jax.nn.sigmoid(c)).astype(BF16)
    o_ref[0] = _dot(a, w_ref[...].astype(BF16)) + b_ref[0]


def _adaln(c, w, b):
    rows, d = c.shape
    n = w.shape[1] // d
    return pl.pallas_call(
        _adaln_body,
        grid=(n,),
        in_specs=[pl.BlockSpec((rows, d), lambda j: (0, 0)),
                  pl.BlockSpec((d, d), lambda j: (0, j)),
                  pl.BlockSpec((1, 1, d), lambda j: (j, 0, 0))],
        out_specs=pl.BlockSpec((1, rows, d), lambda j: (j, 0, 0)),
        out_shape=jax.ShapeDtypeStruct((n, rows, d), F32),
        compiler_params=_cparams(("arbitrary",)),
        name="adaln",
    )(c, w, b.reshape(n, 1, d))


def _group_proj_body(x_ref, g_ref, sh_ref, sc_ref, w_ref, gn_ref, o_ref, h_scr, *, group, tile_kind):
    bb, tt, d = x_ref.shape
    rows = bb * tt
    tn = w_ref.shape[1]
    j = pl.program_id(2)

    @pl.when(j == 0)
    def _():
        h = _norm_mod(x_ref[...], g_ref[...], sh_ref[...], sc_ref[...])
        h_scr[...] = h.reshape(rows, d).astype(BF16)

    y = _dot(h_scr[...], w_ref[...])
    raw = tile_kind(j)

    @pl.when(raw)
    def _():
        o_ref[...] = y.reshape(bb, tt, tn)

    @pl.when(jnp.logical_not(raw))
    def _():
        gn = gn_ref[0]
        lane = lax.broadcasted_iota(jnp.int32, (rows, LANES), 1)
        parts = []
        for c in range(tn // LANES):
            yc = y[:, LANES * c:LANES * (c + 1)]
            sq = yc * yc
            if group == LANES:
                r = _rms_scale(jnp.sum(sq, axis=-1, keepdims=True), group)
            else:
                lo = jnp.sum(jnp.where(lane < group, sq, 0.0), axis=-1, keepdims=True)
                hi = jnp.sum(jnp.where(lane >= group, sq, 0.0), axis=-1, keepdims=True)
                r = jnp.where(lane < group, _rms_scale(lo, group), _rms_scale(hi, group))
            parts.append(yc * r)
        yn = jnp.concatenate(parts, axis=-1) * gn
        o_ref[...] = yn.reshape(bb, tt, tn)


def _group_proj(x, g, shift, scale, w_bf, gains, gain_idx, tile_kind, group, tn, prompt_tile):
    nb, r, d = x.shape
    n = w_bf.shape[1]
    bb, tt = _tok_tiles(x, prompt_tile)
    body = functools.partial(_group_proj_body, group=group, tile_kind=tile_kind)
    return pl.pallas_call(
        body,
        grid=(nb // bb, r // tt, n // tn),
        in_specs=[pl.BlockSpec((bb, tt, d), lambda b, t, j: (b, t, 0)),
                  pl.BlockSpec((1, 1, d), lambda b, t, j: (0, 0, 0)),
                  pl.BlockSpec((bb, 1, d), lambda b, t, j: (b, 0, 0)),
                  pl.BlockSpec((bb, 1, d), lambda b, t, j: (b, 0, 0)),
                  pl.BlockSpec((d, tn), lambda b, t, j: (0, j)),
                  pl.BlockSpec((1, 1, tn), lambda b, t, j: (gain_idx(j), 0, 0))],
        out_specs=pl.BlockSpec((bb, tt, tn), lambda b, t, j: (b, t, j)),
        out_shape=jax.ShapeDtypeStruct((nb, r, n), F32),
        scratch_shapes=[pltpu.VMEM((bb * tt, d), BF16)],
        compiler_params=_cparams(("parallel", "parallel", "arbitrary")),
        name="group_proj",
    )(x, g.reshape(1, 1, d), shift, scale, w_bf, gains)


def _mla_proj_body(x_ref, g_ref, sh_ref, sc_ref, cos_ref, sin_ref, wd_ref, gq_ref, gkv_ref,
                   wuq_ref, wuk_ref, wuv_ref, qn_ref, qp_ref, kn_ref, kp_ref, *outs, sample):
    bb, tt, d = x_ref.shape
    rows = bb * tt
    nh, nope = MLA_HEADS, MLA_NOPE
    dk = nope + MLA_ROPE
    if sample:
        ckv_ref, kpe_ref, a_ref, ape_ref = outs
    else:
        ckv_ref, kpe_ref, q_ref, k_ref, v_ref = outs

    def rope(x, rot):
        x3 = x.reshape(bb, tt, LANES) * cos_ref[...] + rot.reshape(bb, tt, LANES) * sin_ref[...]
        return x3.reshape(rows, LANES)

    h = _norm_mod(x_ref[...], g_ref[...], sh_ref[...], sc_ref[...]).reshape(rows, d).astype(BF16)
    down = _dot(h, wd_ref[...])
    cq = down[:, :MLA_Q_RANK]
    ckv = down[:, MLA_Q_RANK:MLA_Q_RANK + MLA_KV_RANK]
    o = MLA_Q_RANK + MLA_KV_RANK
    kpe = rope(down[:, o:o + LANES], down[:, o + LANES:o + 2 * LANES])
    cq = (cq * _rms_scale(jnp.sum(cq * cq, axis=-1, keepdims=True), MLA_Q_RANK) * gq_ref[...]).astype(BF16)
    ckv = ckv * _rms_scale(jnp.sum(ckv * ckv, axis=-1, keepdims=True), MLA_KV_RANK) * gkv_ref[...]
    ckv_ref[...] = ckv.reshape(bb, tt, MLA_KV_RANK)
    kpe_ref[...] = kpe[:, :MLA_ROPE].reshape(bb, tt, MLA_ROPE)
    ckv_b = ckv.astype(BF16)

    qa = _dot(cq, wuq_ref[...])
    for hh in range(nh):
        qn = qa[:, LANES * hh:LANES * (hh + 1)]
        qp = rope(qa[:, LANES * (nh + hh):LANES * (nh + hh + 1)],
                  qa[:, LANES * (2 * nh + hh):LANES * (2 * nh + hh + 1)])
        r = _rms_scale(jnp.sum(qn * qn, axis=-1, keepdims=True)
                       + jnp.sum(qp * qp, axis=-1, keepdims=True), dk)
        qn = qn * r * qn_ref[...]
        qp = qp * r * qp_ref[...]
        if sample:
            qk = (qn * kn_ref[...]).astype(BF16)
            a = _dot(qk, wuk_ref[hh])
            a_ref[:, :, 2 * LANES * hh:2 * LANES * (hh + 1)] = a.reshape(bb, tt, 2 * LANES)
            ape_ref[:, :, LANES * hh:LANES * (hh + 1)] = (qp * kp_ref[...]).reshape(bb, tt, LANES)
        else:
            q_ref[:, :, 2 * LANES * hh:2 * LANES * hh + LANES] = qn.reshape(bb, tt, LANES).astype(q_ref.dtype)
            q_ref[:, :, 2 * LANES * hh + LANES:2 * LANES * (hh + 1)] = qp.reshape(bb, tt, LANES).astype(q_ref.dtype)

    if not sample:
        kn_all = _dot(ckv_b, wuk_ref[...])
        pe_sq = jnp.sum(kpe * kpe, axis=-1, keepdims=True)
        for hh in range(nh):
            kn = kn_all[:, LANES * hh:LANES * (hh + 1)]
            r = _rms_scale(jnp.sum(kn * kn, axis=-1, keepdims=True) + pe_sq, dk)
            k_ref[:, :, 2 * LANES * hh:2 * LANES * hh + LANES] = (kn * r * kn_ref[...]).reshape(bb, tt, LANES).astype(k_ref.dtype)
            k_ref[:, :, 2 * LANES * hh + LANES:2 * LANES * (hh + 1)] = (kpe * r * kp_ref[...]).reshape(bb, tt, LANES).astype(k_ref.dtype)
        v_ref[...] = _dot(ckv_b, wuv_ref[...]).reshape(bb, tt, nh * LANES).astype(v_ref.dtype)


def _mla_weights(w_dqkv, w_uq, w_uk, w_uv, q_norm, k_norm):
    d = w_dqkv.shape[0]
    half = MLA_ROPE // 2
    nh = MLA_HEADS

    def rot_cols(w):
        return jnp.concatenate([-w[..., half:], w[..., :half]], axis=-1)

    def pad_lanes(w):
        return jnp.pad(w, [(0, 0)] * (w.ndim - 1) + [(0, LANES - w.shape[-1])])

    o = MLA_Q_RANK + MLA_KV_RANK
    w_pe = w_dqkv[:, o:]
    wd = jnp.concatenate([w_dqkv[:, :o], pad_lanes(w_pe), pad_lanes(rot_cols(w_pe))], axis=1).astype(BF16)
    uq_n = w_uq[:, :, :MLA_NOPE].reshape(MLA_Q_RANK, nh * MLA_NOPE)
    uq_p = w_uq[:, :, MLA_NOPE:]
    wuq = jnp.concatenate([uq_n, pad_lanes(uq_p).reshape(MLA_Q_RANK, nh * LANES),
                           pad_lanes(rot_cols(uq_p)).reshape(MLA_Q_RANK, nh * LANES)], axis=1).astype(BF16)

    def gain_pe(gn):
        pe = gn[MLA_NOPE:]
        return pad_lanes(jnp.concatenate([pe, pe])).reshape(1, LANES)

    scale = (MLA_NOPE + MLA_ROPE) ** -0.5
    gains = (q_norm[:MLA_NOPE].reshape(1, MLA_NOPE) * scale, gain_pe(q_norm) * scale,
             k_norm[:MLA_NOPE].reshape(1, MLA_NOPE), gain_pe(k_norm))
    wuk2 = w_uk.reshape(MLA_KV_RANK, nh * MLA_NOPE).astype(BF16)
    wuk_t = jnp.transpose(w_uk, (1, 2, 0)).astype(BF16)
    wuv2 = w_uv.reshape(MLA_KV_RANK, nh * MLA_NOPE).astype(BF16)
    return wd, wuq, wuk2, wuk_t, wuv2, gains


def _rope_tables(pos):
    half = MLA_ROPE // 2
    freqs = jnp.power(ROPE_THETA, -jnp.arange(half, dtype=F32) / half)
    ang = pos.astype(F32)[:, None] * freqs[None, :]
    cos = jnp.tile(jnp.cos(ang), (1, LANES // half))
    sin = jnp.tile(jnp.sin(ang), (1, LANES // half))
    return cos[None], sin[None]


def _mla_proj(x, g, shift, scale, pos, wts, g_q, g_kv, sample, prompt_tile):
    nb, r, d = x.shape
    wd, wuq, wuk2, wuk_t, wuv2, gains = wts
    bb, tt = _tok_tiles(x, prompt_tile)
    cos, sin = _rope_tables(pos)
    nh = MLA_HEADS
    wuk = wuk_t if sample else wuk2
    const = lambda a: pl.BlockSpec(a.shape, lambda b, t: (0,) * a.ndim)
    tok = lambda n: pl.BlockSpec((bb, tt, n), lambda b, t: (b, t, 0))
    mod = pl.BlockSpec((bb, 1, d), lambda b, t: (b, 0, 0))
    tab = pl.BlockSpec((1, tt, LANES), lambda b, t: (0, t, 0))
    gq = g_q.reshape(1, MLA_Q_RANK)
    gkv = g_kv.reshape(1, MLA_KV_RANK)
    g3 = g.reshape(1, 1, d)
    if sample:
        outs = [(MLA_KV_RANK, F32), (MLA_ROPE, F32), (nh * 2 * LANES, F32), (nh * LANES, F32)]
    else:
        outs = [(MLA_KV_RANK, F32), (MLA_ROPE, F32), (nh * 2 * LANES, BF16), (nh * 2 * LANES, BF16), (nh * LANES, BF16)]
    return pl.pallas_call(
        functools.partial(_mla_proj_body, sample=sample),
        grid=(nb // bb, r // tt),
        in_specs=[tok(d), const(g3), mod, mod, tab, tab, const(wd), const(gq), const(gkv),
                  const(wuq), const(wuk), const(wuv2)] + [const(a) for a in gains],
        out_specs=[tok(n) for n, _ in outs],
        out_shape=[jax.ShapeDtypeStruct((nb, r, n), dt) for n, dt in outs],
        compiler_params=_cparams(("parallel", "parallel")),
        name="mla_proj_sample" if sample else "mla_proj",
    )(x, g3, shift, scale, cos, sin, wd, gq, gkv, wuq, wuk, wuv2, *gains)


def _flash_body(q_ref, k_ref, v_ref, sl_ref, o_ref, qs_scr, m_scr, acc_scr, *, n_stack, alibi, split_maps, chunk):
    tq = q_ref.shape[1]
    tk = k_ref.shape[1]
    dv = v_ref.shape[2]
    rows = n_stack * tq
    i = pl.program_id(2)
    j = pl.program_id(3)
    last = ((i + 1) * tq - 1) // tk
    first_diag = (i * tq) // tk

    @pl.when(j == 0)
    def _():
        m_scr[...] = jnp.full(m_scr.shape, NEG_INF, F32)
        acc_scr[...] = jnp.zeros(acc_scr.shape, F32)
        q = q_ref[0]
        if split_maps:
            lane = lax.broadcasted_iota(jnp.int32, (tq, LANES), 1)
            c = 0
            for hh in range(q.shape[1] // LANES):
                qh = q[:, LANES * hh:LANES * (hh + 1)]
                for mp in range(2):
                    keep = (lane < DA_HEAD_DIM) if mp == 0 else (lane >= DA_HEAD_DIM)
                    qs_scr[c * tq:(c + 1) * tq, 0:LANES] = jnp.where(keep, qh, 0.0).astype(BF16)
                    c += 1
            qs_scr[:, LANES:2 * LANES] = sl_ref[0].astype(BF16)
        else:
            qs_scr[...] = q.astype(BF16)

    def step(masked):
        k = k_ref[0].astype(BF16)
        v = v_ref[0].astype(BF16)
        if alibi:
            pos = j * tk + lax.broadcasted_iota(jnp.int32, (tk, LANES), 0)
            lane = lax.broadcasted_iota(jnp.int32, (tk, LANES), 1)
            feat = jnp.where(lane == 0, pos // LANES, jnp.where(lane == 1, pos % LANES, 0))
            k = jnp.concatenate([k, feat.astype(F32).astype(BF16)], axis=-1)
        vx = jnp.concatenate([v, jnp.ones_like(v)], axis=-1)
        nt = tk // LANES
        for c in range(rows // chunk):
            rs = slice(c * chunk, (c + 1) * chunk)
            s = _dot_t(qs_scr[rs, :], k)
            if masked:
                kpos = j * tk + lax.broadcasted_iota(jnp.int32, (chunk, tk), 1)
                qpos = i * tq + (c * chunk) % tq + lax.broadcasted_iota(jnp.int32, (chunk, tk), 0)
                s = jnp.where(kpos <= qpos, s, NEG_INF)
            cols = [s[:, LANES * u:LANES * (u + 1)] for u in range(nt)]
            mx = cols[0]
            for u in range(1, nt):
                mx = jnp.maximum(mx, cols[u])
            m_old = m_scr[rs, :]
            m_new = jnp.maximum(m_old, jnp.broadcast_to(jnp.max(mx, axis=-1, keepdims=True), (chunk, LANES)))
            p = jnp.concatenate([jnp.exp(cu - m_new) for cu in cols], axis=-1).astype(BF16)
            corr = jnp.exp(m_old - m_new)
            acc_scr[rs, :] = acc_scr[rs, :] * jnp.concatenate([corr, corr], axis=-1) + _dot(p, vx)
            m_scr[rs, :] = m_new

    @pl.when(j < first_diag)
    def _():
        step(False)

    @pl.when((j >= first_diag) & (j <= last))
    def _():
        step(True)

    @pl.when(j == pl.num_programs(3) - 1)
    def _():
        acc = acc_scr[...]
        o = acc[:, :dv] / acc[:, dv:]
        for c in range(n_stack):
            o_ref[0, :, dv * c:dv * (c + 1)] = o[c * tq:(c + 1) * tq].astype(o_ref.dtype)


def _flash(q_arr, k_arr, v_arr, slopes, *, n_groups, n_stack, dq, dk, dv, q_col, k_col, v_col,
           tq, tk, alibi, split_maps, out_dtype):
    b, t, _ = q_arr.shape
    tq, tk = min(tq, t), min(tk, t)
    nq, nk = t // tq, t // tk
    rows = n_stack * tq
    chunk = min(FLASH_ROW_CHUNK, rows)
    assert chunk <= tq and tq % chunk == 0

    def kv_idx(col):
        def f(bi, g, i, j):
            return (bi, jnp.minimum(j, ((i + 1) * tq - 1) // tk), col + g)
        return f

    body = functools.partial(_flash_body, n_stack=n_stack, alibi=alibi, split_maps=split_maps, chunk=chunk)
    return pl.pallas_call(
        body,
        grid=(b, n_groups, nq, nk),
        in_specs=[pl.BlockSpec((1, tq, dq), lambda bi, g, i, j: (bi, i, q_col + g)),
                  pl.BlockSpec((1, tk, dk), kv_idx(k_col)),
                  pl.BlockSpec((1, tk, dv), kv_idx(v_col)),
                  pl.BlockSpec((1, rows, LANES), lambda bi, g, i, j: (g, 0, 0))],
        out_specs=pl.BlockSpec((1, tq, n_stack * dv), lambda bi, g, i, j: (bi, i, g)),
        out_shape=jax.ShapeDtypeStruct((b, t, n_groups * n_stack * dv), out_dtype),
        scratch_shapes=[pltpu.VMEM((rows, dk + (LANES if alibi else 0)), BF16), pltpu.VMEM((rows, LANES), F32),
                        pltpu.VMEM((rows, 2 * dv), F32)],
        compiler_params=_cparams(("parallel", "parallel", "parallel", "arbitrary")),
        name="flash",
    )(q_arr, k_arr, v_arr, slopes)


def _band_body(*refs, tile):
    n_g = len(DL_CONFIGS)
    sl_ref = refs[0]
    grp = [refs[1 + 5 * g:1 + 5 * (g + 1)] for g in range(n_g)]
    o_ref = refs[1 + 5 * n_g]
    m_scr, l_scr, acc_scr = refs[2 + 5 * n_g:]
    i = pl.program_id(1)
    nb = N_BACK
    slope = sl_ref[0]
    a_i = lax.broadcasted_iota(jnp.int32, (nb, 2 * nb), 0)
    b_i = lax.broadcasted_iota(jnp.int32, (nb, 2 * nb), 1)
    steps = a_i - b_i + nb
    band = (steps >= 0) & (steps <= nb)
    band_first = band & (b_i >= jnp.where(i == 0, nb, 0))

    for g, (win, dil) in enumerate(DL_CONFIGS):
        q_ref, kc_ref, vc_ref, kp_ref, vp_ref = grp[g]
        n_sub = tile // dil
        bias = -slope[:, :1] * (steps * dil).astype(F32)
        for r in range(dil):
            for qb in range(n_sub // nb):
                sel = pl.ds(r + dil * nb * qb, nb, stride=dil) if dil > 1 else pl.ds(nb * qb, nb)
                q = q_ref[0, sel, :].astype(BF16)
                if qb > 0:
                    psel = pl.ds(r + dil * nb * (qb - 1), nb, stride=dil) if dil > 1 else pl.ds(nb * (qb - 1), nb)
                    k_prev, v_prev = kc_ref[0, psel, :], vc_ref[0, psel, :]
                else:
                    psel = pl.ds(r, nb, stride=dil) if dil > 1 else pl.ds(0, nb)
                    k_prev, v_prev = kp_ref[0, psel, :], vp_ref[0, psel, :]
                k = jnp.concatenate([k_prev, kc_ref[0, sel, :]], axis=0).astype(BF16)
                v = jnp.concatenate([v_prev, vc_ref[0, sel, :]], axis=0).astype(BF16)
                s = _dot_t(q, k) + bias
                s = jnp.where(band_first if qb == 0 else band, s, NEG_INF)
                m_blk = jnp.max(s, axis=-1, keepdims=True)
                if g == 0:
                    m_new = jnp.broadcast_to(m_blk, (nb, LANES))
                    p = jnp.exp(s - m_blk)
                    l_new = jnp.broadcast_to(jnp.sum(p, axis=-1, keepdims=True), (nb, LANES))
                    acc_new = _dot(p.astype(BF16), v)
                else:
                    m_old = m_scr[sel, :]
                    m_new = jnp.maximum(m_old, m_blk)
                    corr = jnp.exp(m_old - m_new)
                    p = jnp.exp(s - m_new[:, :1])
                    l_new = l_scr[sel, :] * corr + jnp.sum(p, axis=-1, keepdims=True)
                    acc_new = acc_scr[sel, :] * corr + _dot(p.astype(BF16), v)
                m_scr[sel, :] = m_new
                l_scr[sel, :] = l_new
                acc_scr[sel, :] = acc_new
    o_ref[0] = (acc_scr[...] / l_scr[...]).astype(o_ref.dtype)


def _band_attention(qkv, slopes_lane, tile, out_dtype):
    b, t, _ = qkv.shape
    nh = DL_HEADS
    tile = min(tile, t)
    in_specs = [pl.BlockSpec((1, 1, LANES), lambda bi, i, h: (h, 0, 0))]
    args = [slopes_lane]
    for g, (win, dil) in enumerate(DL_CONFIGS):
        halo = N_BACK * dil
        per = tile // halo
        col = 3 * nh * g
        in_specs += [
            pl.BlockSpec((1, tile, LANES), lambda bi, i, h, c=col: (bi, i, c + h)),
            pl.BlockSpec((1, tile, LANES), lambda bi, i, h, c=col: (bi, i, c + nh + h)),
            pl.BlockSpec((1, tile, LANES), lambda bi, i, h, c=col: (bi, i, c + 2 * nh + h)),
            pl.BlockSpec((1, halo, LANES), lambda bi, i, h, c=col, p=per: (bi, jnp.maximum(i * p - 1, 0), c + nh + h)),
            pl.BlockSpec((1, halo, LANES), lambda bi, i, h, c=col, p=per: (bi, jnp.maximum(i * p - 1, 0), c + 2 * nh + h)),
        ]
        args += [qkv] * 5
    return pl.pallas_call(
        functools.partial(_band_body, tile=tile),
        grid=(b, t // tile, nh),
        in_specs=in_specs,
        out_specs=pl.BlockSpec((1, tile, LANES), lambda bi, i, h: (bi, i, h)),
        out_shape=jax.ShapeDtypeStruct((b, t, nh * LANES), out_dtype),
        scratch_shapes=[pltpu.VMEM((tile, LANES), F32)] * 3,
        compiler_params=_cparams(("parallel", "parallel", "parallel")),
        name="band_attention",
    )(*args)


def _out_proj_body(*refs, mode, lam_init):
    if mode == "plain":
        a_ref, wo_ref, x_ref, gate_ref, o_ref = refs
    elif mode == "diff":
        a_ref, lq1, lk1, lq2, lk2, sub_ref, wo_ref, x_ref, gate_ref, o_ref = refs
    else:
        o0, o1, o2, e0, e1, e2, wo_ref, x_ref, gate_ref, o_ref = refs
    bb, tt, d = x_ref.shape
    rows = bb * tt
    if mode == "plain":
        a = a_ref[...].reshape(rows, a_ref.shape[2]).astype(BF16)
    elif mode == "diff":
        lam = (jnp.exp(jnp.sum(lq1[...] * lk1[...], axis=-1, keepdims=True))
               - jnp.exp(jnp.sum(lq2[...] * lk2[...], axis=-1, keepdims=True)) + lam_init)
        o = a_ref[...].reshape(rows, a_ref.shape[2]).astype(F32)
        parts = []
        for hh in range(DA_HEADS):
            dlt = o[:, 2 * LANES * hh:2 * LANES * hh + LANES] - lam * o[:, 2 * LANES * hh + LANES:2 * LANES * (hh + 1)]
            r = _rms_scale(jnp.sum(dlt * dlt, axis=-1, keepdims=True), LANES)
            parts.append(dlt * r * sub_ref[...] * (1.0 - lam_init))
        a = jnp.concatenate(parts, axis=-1).astype(BF16)
    else:
        e = [e0[...], e1[...], e2[...]]
        mx = jnp.maximum(jnp.maximum(e[0], e[1]), e[2])
        w = [jnp.exp(ei - mx) for ei in e]
        den = w[0] + w[1] + w[2]
        a = (w[0] * o0[...] + w[1] * o1[...] + w[2] * o2[...]) / den
        a = a.reshape(rows, d).astype(BF16)
    y = _dot(a, wo_ref[...])
    o_ref[...] = x_ref[...] + gate_ref[...] * y.reshape(bb, tt, d)


def _out_proj(acts, extras, wo_bf, x, gate, mode, lam_init, prompt_tile):
    nb, r, d = x.shape
    bb, tt = _tok_tiles(x, prompt_tile)
    tok = lambda n: pl.BlockSpec((bb, tt, n), lambda b, t: (b, t, 0))
    const = lambda a: pl.BlockSpec(a.shape, lambda b, t: (0,) * a.ndim)
    in_specs = [tok(a.shape[2]) for a in acts] + [const(e) for e in extras] + [
        const(wo_bf), tok(d), pl.BlockSpec((bb, 1, d), lambda b, t: (b, 0, 0))]
    return pl.pallas_call(
        functools.partial(_out_proj_body, mode=mode, lam_init=lam_init),
        grid=(nb // bb, r // tt),
        in_specs=in_specs,
        out_specs=tok(d),
        out_shape=jax.ShapeDtypeStruct((nb, r, d), F32),
        compiler_params=_cparams(("parallel", "parallel")),
        name="out_proj_" + mode,
    )(*acts, *extras, wo_bf, x, gate)


def _ffn_body(x_ref, g_ref, sh_ref, sc_ref, gate_ref, w1_ref, w3_ref, w2_ref, o_ref, h_scr, acc_scr):
    bb, tt, d = x_ref.shape
    rows = bb * tt
    j = pl.program_id(2)

    @pl.when(j == 0)
    def _():
        h = _norm_mod(x_ref[...], g_ref[...], sh_ref[...], sc_ref[...])
        h_scr[...] = h.reshape(rows, d).astype(BF16)
        acc_scr[...] = jnp.zeros(acc_scr.shape, F32)

    h = h_scr[...]
    a = _dot(h, w1_ref[...])
    b = _dot(h, w3_ref[...])
    u = (a * jax.nn.sigmoid(a) * b).astype(BF16)
    acc_scr[...] += _dot(u, w2_ref[...])

    @pl.when(j == pl.num_programs(2) - 1)
    def _():
        o_ref[...] = x_ref[...] + gate_ref[...] * acc_scr[...].reshape(bb, tt, d)


def _ffn(x, g, shift, scale, gate, w1, w3, w2, th, prompt_tile):
    nb, r, d = x.shape
    hid = w1.shape[1]
    bb, tt = _tok_tiles(x, prompt_tile)
    mod = pl.BlockSpec((bb, 1, d), lambda b, t, j: (b, 0, 0))
    return pl.pallas_call(
        _ffn_body,
        grid=(nb // bb, r // tt, hid // th),
        in_specs=[pl.BlockSpec((bb, tt, d), lambda b, t, j: (b, t, 0)),
                  pl.BlockSpec((1, 1, d), lambda b, t, j: (0, 0, 0)),
                  mod, mod, mod,
                  pl.BlockSpec((d, th), lambda b, t, j: (0, j)),
                  pl.BlockSpec((d, th), lambda b, t, j: (0, j)),
                  pl.BlockSpec((th, d), lambda b, t, j: (j, 0))],
        out_specs=pl.BlockSpec((bb, tt, d), lambda b, t, j: (b, t, 0)),
        out_shape=jax.ShapeDtypeStruct((nb, r, d), F32),
        scratch_shapes=[pltpu.VMEM((bb * tt, d), BF16), pltpu.VMEM((bb * tt, d), F32)],
        compiler_params=_cparams(("parallel", "parallel", "arbitrary")),
        name="ffn",
    )(x, g.reshape(1, 1, d), shift, scale, gate, w1, w3, w2)


def _softmax_update(s, m_scr, l_scr):
    m_old = m_scr[...]
    m_new = jnp.maximum(m_old, jnp.max(s, axis=-1, keepdims=True))
    corr = jnp.exp(m_old - m_new)
    p = jnp.exp(s - m_new)
    l_scr[...] = l_scr[...] * corr + jnp.sum(p, axis=-1, keepdims=True)
    m_scr[...] = m_new
    return p, corr


def _mla_decode_body(pt_ref, a_ref, ape_ref, ckv_ref, kpet_ref, wuk_ref, wuv_ref, *rest, pages, per):
    lat_refs = rest[:pages]
    pet_refs = rest[pages:2 * pages]
    o_ref = rest[2 * pages]
    aq_scr, apq_scr, m_scr, l_scr, acc_scr, latn_scr = rest[2 * pages + 1:]
    nh = MLA_HEADS
    s_tok = a_ref.shape[1]
    nrow = nh * s_tok
    c = pl.program_id(1)
    dk = MLA_NOPE + MLA_ROPE

    @pl.when(c == 0)
    def _():
        m_scr[...] = jnp.full(m_scr.shape, NEG_INF, F32)
        l_scr[...] = jnp.zeros(l_scr.shape, F32)
        acc_scr[...] = jnp.zeros(acc_scr.shape, F32)
        for hh in range(nh):
            aq_scr[s_tok * hh:s_tok * (hh + 1), :] = a_ref[0, :, 2 * LANES * hh:2 * LANES * (hh + 1)].astype(BF16)
            apq_scr[s_tok * hh:s_tok * (hh + 1), :] = ape_ref[0, :, LANES * hh:LANES * hh + MLA_ROPE].astype(BF16)

    def scores(latb, pet):
        nk = latb.shape[0]
        kt = _dot_t(wuk_ref[...], latb)
        pe_sq = jnp.sum(pet * pet, axis=0, keepdims=True)
        ssq = []
        for hh in range(nh):
            kh = kt[MLA_NOPE * hh:MLA_NOPE * (hh + 1), :]
            ssq.append(jnp.broadcast_to(jnp.sum(kh * kh, axis=0, keepdims=True) + pe_sq, (s_tok, nk)))
        r = _rms_scale(jnp.concatenate(ssq, axis=0), dk)
        return (_dot_t(aq_scr[...], latb) + _dot(apq_scr[...], pet.astype(BF16))) * r

    def absorb(s, latb):
        p, corr = _softmax_update(s, m_scr, l_scr)
        acc_scr[...] = acc_scr[...] * corr + _dot(p.astype(BF16), latb)

    n_sub = pages // per
    latb = [jnp.concatenate([lat_refs[u * per + w][0] for w in range(per)], axis=0).astype(BF16) for u in range(n_sub)]
    pet = [jnp.concatenate([pet_refs[u * per + w][0] for w in range(per)], axis=1) for u in range(n_sub)]
    absorb(jnp.concatenate([scores(latb[u], pet[u]) for u in range(n_sub)], axis=1), jnp.concatenate(latb, axis=0))

    @pl.when(c == pl.num_programs(1) - 1)
    def _():
        latn_scr[...] = jnp.zeros(latn_scr.shape, F32)
        latn_scr[0:s_tok, :] = ckv_ref[0]
        qi = lax.broadcasted_iota(jnp.int32, (nrow, LANES), 0) % s_tok
        kj = lax.broadcasted_iota(jnp.int32, (nrow, LANES), 1)
        latn = latn_scr[...].astype(BF16)
        absorb(jnp.where(kj <= qi, scores(latn, kpet_ref[0]), NEG_INF), latn)
        o_lat = (acc_scr[...] / l_scr[...]).astype(BF16)
        for hh in range(nh):
            o_ref[0, :, LANES * hh:LANES * (hh + 1)] = _dot(
                o_lat[s_tok * hh:s_tok * (hh + 1), :], wuv_ref[:, LANES * hh:LANES * (hh + 1)])


def _mla_decode(a, ape, ckv_s, kpe_s, cache_lat, cache_kpe, page_table, wuk_rows, wuv2, pages, per):
    db, s_tok, _ = a.shape
    n_pages = page_table.shape[1]
    page = cache_lat.shape[1]
    pages = min(pages, n_pages)
    nh = MLA_HEADS
    pt = page_table.reshape(-1)
    cache_pet = jnp.swapaxes(cache_kpe, 1, 2)
    kpet_s = jnp.pad(jnp.swapaxes(kpe_s, 1, 2), ((0, 0), (0, 0), (0, LANES - s_tok)))

    def page_spec(shape, jj):
        return pl.BlockSpec((1,) + shape, lambda b, c, pt_ref: (pt_ref[b * n_pages + c * pages + jj], 0, 0))

    seq = lambda arr: pl.BlockSpec((1,) + arr.shape[1:], lambda b, c, pt_ref: (b, 0, 0))
    const = lambda arr: pl.BlockSpec(arr.shape, lambda b, c, pt_ref: (0,) * arr.ndim)
    grid_spec = pltpu.PrefetchScalarGridSpec(
        num_scalar_prefetch=1,
        grid=(db, n_pages // pages),
        in_specs=[seq(a), seq(ape), seq(ckv_s), seq(kpet_s), const(wuk_rows), const(wuv2)]
                 + [page_spec((page, MLA_KV_RANK), jj) for jj in range(pages)]
                 + [page_spec((MLA_ROPE, page), jj) for jj in range(pages)],
        out_specs=pl.BlockSpec((1, s_tok, nh * LANES), lambda b, c, pt_ref: (b, 0, 0)),
        scratch_shapes=[pltpu.VMEM((nh * s_tok, MLA_KV_RANK), BF16), pltpu.VMEM((nh * s_tok, MLA_ROPE), BF16),
                        pltpu.VMEM((nh * s_tok, 1), F32), pltpu.VMEM((nh * s_tok, 1), F32),
                        pltpu.VMEM((nh * s_tok, MLA_KV_RANK), F32),
                        pltpu.VMEM((LANES, MLA_KV_RANK), F32)],
    )
    return pl.pallas_call(
        functools.partial(_mla_decode_body, pages=pages, per=per),
        grid_spec=grid_spec,
        out_shape=jax.ShapeDtypeStruct((db, s_tok, nh * LANES), F32),
        compiler_params=_cparams(("parallel", "arbitrary")),
        name="mla_decode",
    )(pt, a, ape, ckv_s, kpet_s, wuk_rows, wuv2, *([cache_lat] * pages), *([cache_pet] * pages))


def _diff_decode_body(pt_ref, qkv_ref, sl_ref, *rest, pages, past):
    kt_refs = rest[:pages]
    v_refs = rest[pages:2 * pages]
    o_ref = rest[2 * pages]
    qz_scr, m_scr, l_scr, acc_scr, kn_scr, vn_scr = rest[2 * pages + 1:]
    s_tok = qkv_ref.shape[1]
    nkv = DA_KV_HEADS
    per_kv = (DA_HEADS // nkv) * 2 * s_tok
    nrow = nkv * per_kv
    page = kt_refs[0].shape[2]
    c = pl.program_id(1)
    nq = DA_HEADS * 2 * DA_HEAD_DIM

    @pl.when(c == 0)
    def _():
        m_scr[...] = jnp.full(m_scr.shape, NEG_INF, F32)
        l_scr[...] = jnp.zeros(l_scr.shape, F32)
        acc_scr[...] = jnp.zeros(acc_scr.shape, F32)
        lane = lax.broadcasted_iota(jnp.int32, (s_tok, LANES), 1)
        row = 0
        for hh in range(DA_HEADS):
            qh = qkv_ref[0, :, LANES * hh:LANES * (hh + 1)]
            for mp in range(2):
                keep = (lane < DA_HEAD_DIM) if mp == 0 else (lane >= DA_HEAD_DIM)
                qz_scr[row:row + s_tok, :] = jnp.where(keep, qh, 0.0).astype(BF16)
                row += s_tok

    def absorb(s, vs, kpos0, mask):
        nk = s.shape[1]
        rel = kpos0 + lax.broadcasted_iota(jnp.int32, (1, nk), 1)
        s = s + sl_ref[...] * rel.astype(F32)
        if mask is not None:
            s = jnp.where(mask, s, NEG_INF)
        p, corr = _softmax_update(s, m_scr, l_scr)
        pb = p.astype(BF16)
        pv = jnp.concatenate([_dot(pb[per_kv * kv:per_kv * (kv + 1), :], vs[kv]) for kv in range(nkv)], axis=0)
        acc_scr[...] = acc_scr[...] * corr + pv

    def qz(kv):
        return qz_scr[per_kv * kv:per_kv * (kv + 1), :]

    kt = jnp.concatenate([kt_refs[w][0] for w in range(pages)], axis=1).astype(BF16)
    s = jnp.concatenate([_dot(qz(kv), kt[LANES * kv:LANES * (kv + 1), :]) for kv in range(nkv)], axis=0)
    vs = [jnp.concatenate([v_refs[w][0, pl.ds(kv, page, stride=nkv), :] for w in range(pages)], axis=0).astype(BF16)
          for kv in range(nkv)]
    absorb(s, vs, c * pages * page - past, None)

    @pl.when(c == pl.num_programs(1) - 1)
    def _():
        kn_scr[...] = jnp.zeros(kn_scr.shape, F32)
        vn_scr[...] = jnp.zeros(vn_scr.shape, F32)
        kn_scr[0:s_tok, :] = qkv_ref[0, :, nq:nq + nkv * LANES]
        vn_scr[0:s_tok, :] = qkv_ref[0, :, nq + nkv * LANES:nq + 2 * nkv * LANES]
        qi = lax.broadcasted_iota(jnp.int32, (nrow, LANES), 0) % s_tok
        kj = lax.broadcasted_iota(jnp.int32, (nrow, LANES), 1)
        kn = kn_scr[...].astype(BF16)
        vn = vn_scr[...].astype(BF16)
        s = jnp.concatenate([_dot_t(qz(kv), kn[:, LANES * kv:LANES * (kv + 1)]) for kv in range(nkv)], axis=0)
        absorb(s, [vn[:, LANES * kv:LANES * (kv + 1)] for kv in range(nkv)], 0, kj <= qi)
        o_ref[0] = acc_scr[...] / l_scr[...]


def _diff_decode(qkv_s, cache_k, cache_v, page_table, slopes_rows, pages):
    db, s_tok, _ = qkv_s.shape
    n_pages = page_table.shape[1]
    n_pool, page = cache_k.shape[:2]
    pages = min(pages, n_pages)
    width = DA_KV_HEADS * LANES
    nrow = DA_HEADS * 2 * s_tok
    pt = page_table.reshape(-1)
    cache_kt = jnp.transpose(cache_k, (0, 2, 3, 4, 1)).reshape(n_pool, width, page)
    cache_v2 = cache_v.reshape(n_pool, page * DA_KV_HEADS, LANES)

    def kt_spec(jj):
        return pl.BlockSpec((1, width, page), lambda b, c, pt_ref: (pt_ref[b * n_pages + c * pages + jj], 0, 0))

    def v_spec(jj):
        return pl.BlockSpec((1, page * DA_KV_HEADS, LANES), lambda b, c, pt_ref: (pt_ref[b * n_pages + c * pages + jj], 0, 0))

    grid_spec = pltpu.PrefetchScalarGridSpec(
        num_scalar_prefetch=1,
        grid=(db, n_pages // pages),
        in_specs=[pl.BlockSpec((1, s_tok, qkv_s.shape[2]), lambda b, c, pt_ref: (b, 0, 0)),
                  pl.BlockSpec((nrow, 1), lambda b, c, pt_ref: (0, 0))]
                 + [kt_spec(jj) for jj in range(pages)] + [v_spec(jj) for jj in range(pages)],
        out_specs=pl.BlockSpec((1, nrow, LANES), lambda b, c, pt_ref: (b, 0, 0)),
        scratch_shapes=[pltpu.VMEM((nrow, LANES), BF16), pltpu.VMEM((nrow, 1), F32), pltpu.VMEM((nrow, 1), F32),
                        pltpu.VMEM((nrow, LANES), F32),
                        pltpu.VMEM((LANES, width), F32), pltpu.VMEM((LANES, width), F32)],
    )
    return pl.pallas_call(
        functools.partial(_diff_decode_body, pages=pages, past=n_pages * page),
        grid_spec=grid_spec,
        out_shape=jax.ShapeDtypeStruct((db, nrow, LANES), F32),
        compiler_params=_cparams(("parallel", "arbitrary")),
        name="diff_decode",
    )(pt, qkv_s, slopes_rows, *([cache_kt] * pages), *([cache_v2] * pages))


def _dil_decode_body(qkv_ref, sl_ref, buf_ref, o_ref, lse_ref, roll_ref, kn_scr, vn_scr, new_scr, sem, *, dil, n_res):
    s_tok = qkv_ref.shape[1]
    nh = DL_HEADS
    hd = nh * DL_HEAD_DIM
    nrow = nh * s_tok
    per_entry = 2 * nh
    rows = buf_ref.shape[1]
    nb = rows // (per_entry * dil)
    b = pl.program_id(0)

    keep = rows - s_tok * per_entry
    shift = pltpu.make_async_copy(buf_ref.at[0, pl.ds(s_tok * per_entry, keep)], roll_ref.at[b, pl.ds(0, keep)], sem.at[0])
    shift.start()
    for kv in range(2):
        for hh in range(nh):
            lo = hd * (1 + kv) + DL_HEAD_DIM * hh
            new_scr[pl.ds(kv * nh + hh, s_tok, stride=per_entry), :] = qkv_ref[0, :, lo:lo + DL_HEAD_DIM]
    append = pltpu.make_async_copy(new_scr, roll_ref.at[b, pl.ds(keep, s_tok * per_entry)], sem.at[1])
    append.start()

    q = qkv_ref[0, :, :hd]
    row_h = lax.broadcasted_iota(jnp.int32, (nrow, hd), 0) // s_tok
    col_h = lax.broadcasted_iota(jnp.int32, (nrow, hd), 1) // DL_HEAD_DIM
    qbd = jnp.where(row_h == col_h, jnp.concatenate([q] * nh, axis=0), 0.0).astype(BF16)
    kn_scr[...] = jnp.zeros(kn_scr.shape, F32)
    vn_scr[...] = jnp.zeros(vn_scr.shape, F32)
    kn_scr[0:s_tok, :] = qkv_ref[0, :, hd:2 * hd]
    vn_scr[0:s_tok, :] = qkv_ref[0, :, 2 * hd:3 * hd]

    qi = lax.broadcasted_iota(jnp.int32, (nrow, nb), 0) % s_tok
    kj = lax.broadcasted_iota(jnp.int32, (nrow, nb), 1)
    q_res, q_t = qi % dil, qi // dil
    slope = sl_ref[...]

    def gather(r, kv):
        return jnp.concatenate(
            [buf_ref[0, pl.ds((r * 2 + kv) * nh + hh, nb, stride=per_entry * dil), :] for hh in range(nh)],
            axis=-1).astype(BF16)

    blocks = []
    for r in range(n_res):
        steps = nb + q_t - kj
        s = _dot_t(qbd, gather(r, 0)) - slope * (steps * dil).astype(F32)
        blocks.append((jnp.where((q_res == r) & (kj >= q_t), s, NEG_INF), gather(r, 1)))
    s = _dot_t(qbd, kn_scr[...].astype(BF16)) - slope * (qi - kj).astype(F32)
    ok = (kj < s_tok) & (kj <= qi) & (kj % dil == q_res)
    blocks.append((jnp.where(ok, s, NEG_INF), vn_scr[...].astype(BF16)))

    m = blocks[0][0].max(axis=-1, keepdims=True)
    for s, _ in blocks[1:]:
        m = jnp.maximum(m, s.max(axis=-1, keepdims=True))
    l = jnp.zeros((nrow, 1), F32)
    acc = jnp.zeros((nrow, hd), F32)
    for s, v in blocks:
        p = jnp.exp(s - m)
        l = l + jnp.sum(p, axis=-1, keepdims=True)
        acc = acc + _dot(p.astype(BF16), v)
    out = acc / l
    lse = m + jnp.log(l)
    for hh in range(nh):
        rs = slice(s_tok * hh, s_tok * (hh + 1))
        cs = slice(DL_HEAD_DIM * hh, DL_HEAD_DIM * (hh + 1))
        o_ref[0, :, cs] = out[rs, cs]
        lse_ref[0, :, cs] = jnp.broadcast_to(lse[rs], (s_tok, DL_HEAD_DIM))
    shift.wait()
    append.wait()


def _dil_decode(qkv_s, buf, slopes_rows, g, dil):
    db, s_tok, _ = qkv_s.shape
    nh = DL_HEADS
    hd = nh * DL_HEAD_DIM
    wb = buf.shape[1]
    assert wb == N_BACK * dil, "window buffer must hold the whole window"
    n_res = min(dil, s_tok)
    rows = wb * 2 * nh
    buf2 = buf.reshape(db, rows, DL_HEAD_DIM)
    out = jax.ShapeDtypeStruct((db, s_tok, hd), F32)
    o_g, lse_g, rolled = pl.pallas_call(
        functools.partial(_dil_decode_body, dil=dil, n_res=n_res),
        grid=(db,),
        in_specs=[pl.BlockSpec((1, s_tok, 3 * hd), lambda b: (b, 0, g)),
                  pl.BlockSpec((nh * s_tok, 1), lambda b: (0, 0)),
                  pl.BlockSpec((1, rows, DL_HEAD_DIM), lambda b: (b, 0, 0))],
        out_specs=[pl.BlockSpec((1, s_tok, hd), lambda b: (b, 0, 0))] * 2 + [pl.BlockSpec(memory_space=pl.ANY)],
        out_shape=[out, out, jax.ShapeDtypeStruct(buf2.shape, buf.dtype)],
        scratch_shapes=[pltpu.VMEM((LANES, hd), F32), pltpu.VMEM((LANES, hd), F32),
                        pltpu.VMEM((s_tok * 2 * nh, DL_HEAD_DIM), F32), pltpu.SemaphoreType.DMA((2,))],
        compiler_params=_cparams(("arbitrary",)),
        name="dil_decode",
    )(qkv_s, slopes_rows, buf2)
    return o_g, lse_g, rolled.reshape(buf.shape)


PROMPT_TILE = 512
FFN_TILE = 1024
FFN_HIDDEN_TILE = 256
DECODE_PAGES = 8
DECODE_SUB_PAGES = 2
FLASH_ROW_CHUNK = 256


def _alibi_slopes(n):
    return np.array([2.0 ** (-8.0 * (i + 1) / n) for i in range(n)], dtype=np.float32)


def _mla_layer(hp_args, hs_args, caches, page_table, params):
    w_dqkv, g_q, g_kv, w_uq, w_uk, w_uv, q_norm, k_norm, w_o = params
    xp, gmix, shp, scp = hp_args
    xs, _, shs, scs = hs_args
    cache_lat, cache_kpe = caches
    wts = _mla_weights(w_dqkv, w_uq, w_uk, w_uv, q_norm, k_norm)
    b, t, _ = xp.shape
    db, s_tok, _ = xs.shape
    past = page_table.shape[1] * cache_lat.shape[1]
    ckv_p, kpe_p, q, k, v = _mla_proj(xp, gmix, shp, scp, jnp.arange(t), wts, g_q, g_kv, False, PROMPT_TILE)
    nh = MLA_HEADS
    o_p = _flash(q, k, v, jnp.zeros((nh, min(1024, t), LANES), F32), n_groups=nh, n_stack=1,
                 dq=2 * LANES, dk=2 * LANES, dv=LANES, q_col=0, k_col=0, v_col=0,
                 tq=1024, tk=1024, alibi=False, split_maps=False, out_dtype=BF16)
    ckv_s, kpe_s, a, ape = _mla_proj(xs, gmix, shs, scs, past + jnp.arange(s_tok), wts, g_q, g_kv, True, PROMPT_TILE)
    wuk_rows = wts[3].reshape(nh * MLA_NOPE, MLA_KV_RANK)
    o_s = _mla_decode(a, ape, ckv_s, kpe_s, cache_lat, cache_kpe, page_table, wuk_rows, wts[4],
                      DECODE_PAGES, DECODE_SUB_PAGES)
    return (o_p,), (o_s,), "plain", (), w_o, (ckv_p, ckv_s, kpe_p, kpe_s)


def _diff_layer(hp_args, hs_args, caches, page_table, layer_idx, params):
    w_qkv, q_norm, k_norm, lq1, lk1, lq2, lk2, subln, w_o = params
    xp, gmix, shp, scp = hp_args
    xs, _, shs, scs = hs_args
    cache_k, cache_v = caches
    b, t, _ = xp.shape
    db, s_tok, _ = xs.shape
    scale = DA_HEAD_DIM ** -0.5
    tn = 512
    rep = tn // DA_HEAD_DIM
    gains = jnp.stack([jnp.tile(q_norm, rep) * scale, jnp.tile(k_norm, rep), jnp.ones((tn,), F32)]).reshape(3, 1, tn)
    nq_tiles = DA_HEADS * 2 * DA_HEAD_DIM // tn
    nk_tiles = DA_KV_HEADS * 2 * DA_HEAD_DIM // tn
    gain_idx = lambda j: jnp.maximum(j - (nq_tiles - 1), 0)
    kind = lambda j: j >= nq_tiles + nk_tiles
    wq = w_qkv.astype(BF16)
    qkv_p = _group_proj(xp, gmix, shp, scp, wq, gains, gain_idx, kind, DA_HEAD_DIM, tn, FFN_TILE)
    qkv_s = _group_proj(xs, gmix, shs, scs, wq, gains, gain_idx, kind, DA_HEAD_DIM, tn, PROMPT_TILE)
    slopes = _alibi_slopes(DA_HEADS)
    tq = min(256, t)
    per_kv = DA_HEADS // DA_KV_HEADS
    sl_rows = np.repeat(slopes.reshape(DA_KV_HEADS, per_kv), 2 * tq, axis=1).reshape(DA_KV_HEADS, per_kv * 2 * tq, 1)
    sl_feat = np.zeros(sl_rows.shape[:2] + (LANES,), np.float32)
    sl_feat[:, :, 0:1] = sl_rows * LANES
    sl_feat[:, :, 1:2] = sl_rows
    assert np.array_equal(sl_feat, sl_feat.astype(BF16).astype(np.float32)), "slopes must be exact in bf16"
    kc = DA_HEADS * 2 * DA_HEAD_DIM // LANES
    o_p = _flash(qkv_p, qkv_p, qkv_p, jnp.asarray(sl_feat), n_groups=DA_KV_HEADS, n_stack=per_kv * 2,
                 dq=per_kv * LANES, dk=LANES, dv=LANES, q_col=0, k_col=kc, v_col=kc + DA_KV_HEADS,
                 tq=tq, tk=1024, alibi=True, split_maps=True, out_dtype=BF16)
    sl_dec = jnp.asarray(np.repeat(slopes, 2 * s_tok).reshape(DA_HEADS * 2 * s_tok, 1))
    o_dec = _diff_decode(qkv_s, cache_k, cache_v, page_table, sl_dec, DECODE_PAGES)
    o_s = jnp.transpose(o_dec.reshape(db, DA_HEADS * 2, s_tok, LANES), (0, 2, 1, 3)).reshape(db, s_tok, DA_HEADS * 2 * LANES)
    nq = DA_HEADS * 2 * DA_HEAD_DIM
    nk = DA_KV_HEADS * 2 * DA_HEAD_DIM
    k_p = qkv_p[..., nq:nq + nk].reshape(b, t, DA_KV_HEADS, 2, DA_HEAD_DIM)
    k_s = qkv_s[..., nq:nq + nk].reshape(db, s_tok, DA_KV_HEADS, 2, DA_HEAD_DIM)
    v_p = qkv_p[..., nq + nk:].reshape(b, t, DA_KV_HEADS, 2 * DA_HEAD_DIM)
    v_s = qkv_s[..., nq + nk:].reshape(db, s_tok, DA_KV_HEADS, 2 * DA_HEAD_DIM)
    extras = tuple(a.reshape(1, -1) for a in (lq1, lk1, lq2, lk2, subln))
    return (o_p,), (o_s,), "diff", extras, w_o, (k_p, k_s, v_p, v_s)


def _dilated_layer(hp_args, hs_args, caches, params):
    w_qkv, q_norm, k_norm, w_o = params
    xp, gmix, shp, scp = hp_args
    xs, _, shs, scs = hs_args
    b, t, _ = xp.shape
    db, s_tok, _ = xs.shape
    nh, hd = DL_HEADS, DL_HEAD_DIM
    scale = hd ** -0.5
    tn = nh * hd
    gains = jnp.stack([jnp.tile(q_norm, nh) * scale, jnp.tile(k_norm, nh), jnp.ones((tn,), F32)]).reshape(3, 1, tn)
    wq = w_qkv.astype(BF16)
    gain_idx = lambda j: j % 3
    kind = lambda j: j % 3 == 2
    qkv_p = _group_proj(xp, gmix, shp, scp, wq, gains, gain_idx, kind, hd, tn, FFN_TILE)
    qkv_s = _group_proj(xs, gmix, shs, scs, wq, gains, gain_idx, kind, hd, tn, PROMPT_TILE)
    slopes = _alibi_slopes(nh)
    sl_lane = jnp.asarray(np.repeat(slopes, LANES).reshape(nh, 1, LANES))
    o_p = _band_attention(qkv_p, sl_lane, 2048, BF16)
    sl_rows = jnp.asarray(np.repeat(slopes, s_tok).reshape(nh * s_tok, 1))
    outs, lses, st = [], [], []
    qkv_p6 = qkv_p.reshape(b, t, len(DL_CONFIGS), 3, nh, hd)
    for g, (win, dil) in enumerate(DL_CONFIGS):
        o_g, lse_g, rolled = _dil_decode(qkv_s, caches[g], sl_rows, g, dil)
        outs.append(o_g)
        lses.append(lse_g)
        wb = min(win, t)
        st.append(qkv_p6[:, t - wb:, g, 1:3])
        st.append(rolled)
    return (o_p,), tuple(outs) + tuple(lses), ("plain", "dil"), (), w_o, tuple(st)


def kernel(x_prompt, x_sample, cache_l0_latent, cache_l0_kpe, cache_l1_k, cache_l1_v, cache_l2_kv_w128, cache_l2_kv_w512, cache_l2_kv_w2048, cache_l3_latent, cache_l3_kpe, page_table, c_prompt, c_sample, l0_ada_w, l0_ada_b, l0_norm_mix, l0_w_dqkv, l0_g_q, l0_g_kv, l0_w_uq, l0_w_uk, l0_w_uv, l0_q_norm, l0_k_norm, l0_w_o, l0_norm_ffn, l0_ffn_w1, l0_ffn_w3, l0_ffn_w2, l1_ada_w, l1_ada_b, l1_norm_mix, l1_w_qkv, l1_q_norm, l1_k_norm, l1_lambda_q1, l1_lambda_k1, l1_lambda_q2, l1_lambda_k2, l1_subln, l1_w_o, l1_norm_ffn, l1_ffn_w1, l1_ffn_w3, l1_ffn_w2, l2_ada_w, l2_ada_b, l2_norm_mix, l2_w_qkv, l2_q_norm, l2_k_norm, l2_w_o, l2_norm_ffn, l2_ffn_w1, l2_ffn_w3, l2_ffn_w2, l3_ada_w, l3_ada_b, l3_norm_mix, l3_w_dqkv, l3_g_q, l3_g_kv, l3_w_uq, l3_w_uk, l3_w_uv, l3_q_norm, l3_k_norm, l3_w_o, l3_norm_ffn, l3_ffn_w1, l3_ffn_w3, l3_ffn_w2):
    block = [
        (l0_ada_w, l0_ada_b, l0_norm_mix, l0_norm_ffn, l0_ffn_w1, l0_ffn_w3, l0_ffn_w2),
        (l1_ada_w, l1_ada_b, l1_norm_mix, l1_norm_ffn, l1_ffn_w1, l1_ffn_w3, l1_ffn_w2),
        (l2_ada_w, l2_ada_b, l2_norm_mix, l2_norm_ffn, l2_ffn_w1, l2_ffn_w3, l2_ffn_w2),
        (l3_ada_w, l3_ada_b, l3_norm_mix, l3_norm_ffn, l3_ffn_w1, l3_ffn_w3, l3_ffn_w2),
    ]
    mixer = [
        (l0_w_dqkv, l0_g_q, l0_g_kv, l0_w_uq, l0_w_uk, l0_w_uv, l0_q_norm, l0_k_norm, l0_w_o),
        (l1_w_qkv, l1_q_norm, l1_k_norm, l1_lambda_q1, l1_lambda_k1, l1_lambda_q2, l1_lambda_k2, l1_subln, l1_w_o),
        (l2_w_qkv, l2_q_norm, l2_k_norm, l2_w_o),
        (l3_w_dqkv, l3_g_q, l3_g_kv, l3_w_uq, l3_w_uk, l3_w_uv, l3_q_norm, l3_k_norm, l3_w_o),
    ]
    caches = [
        (cache_l0_latent, cache_l0_kpe),
        (cache_l1_k, cache_l1_v),
        (cache_l2_kv_w128, cache_l2_kv_w512, cache_l2_kv_w2048),
        (cache_l3_latent, cache_l3_kpe),
    ]
    b, t, d = x_prompt.shape
    db, s_tok, _ = x_sample.shape
    rows = b + db
    rows_pad = -(-rows // 8) * 8
    c_all = jnp.pad(jnp.concatenate([c_prompt, c_sample], axis=0), ((0, rows_pad - rows), (0, 0)))
    xp, xs = x_prompt, x_sample
    states = []
    for i in range(len(block)):
        ada_w, ada_b, g_mix, g_ffn, w1, w3, w2 = block[i]
        mod = _adaln(c_all, ada_w, ada_b)
        mod_p = mod[:, :b].reshape(6, b, 1, d)
        mod_s = mod[:, b:rows].reshape(6, db, 1, d)
        hp_args = (xp, g_mix, mod_p[0], mod_p[1])
        hs_args = (xs, g_mix, mod_s[0], mod_s[1])
        kind = i % 3
        lam_init = 0.0
        if kind == 0:
            a_p, a_s, mode, extras, w_o, st = _mla_layer(hp_args, hs_args, caches[i], page_table, mixer[i])
        elif kind == 1:
            lam_init = 0.8 - 0.6 * math.exp(-0.3 * i)
            a_p, a_s, mode, extras, w_o, st = _diff_layer(hp_args, hs_args, caches[i], page_table, i, mixer[i])
        else:
            a_p, a_s, mode, extras, w_o, st = _dilated_layer(hp_args, hs_args, caches[i], mixer[i])
        mode_p, mode_s = mode if isinstance(mode, tuple) else (mode, mode)
        wo_bf = w_o.astype(BF16)
        xp = _out_proj(a_p, extras, wo_bf, xp, mod_p[2], mode_p, lam_init, PROMPT_TILE)
        xs = _out_proj(a_s, extras, wo_bf, xs, mod_s[2], mode_s, lam_init, PROMPT_TILE)
        w1b, w3b, w2b = w1.astype(BF16), w3.astype(BF16), w2.astype(BF16)
        xp = _ffn(xp, g_ffn, mod_p[3], mod_p[4], mod_p[5], w1b, w3b, w2b, FFN_HIDDEN_TILE, FFN_TILE)
        xs = _ffn(xs, g_ffn, mod_s[3], mod_s[4], mod_s[5], w1b, w3b, w2b, FFN_HIDDEN_TILE, FFN_TILE)
        states.append(st)
    l0, l1, l2, l3 = states
    return (xp, xs, l0[0], l0[1], l0[2], l0[3], l1[0], l1[1], l1[2], l1[3],
            l2[0], l2[1], l2[2], l2[3], l2[4], l2[5], l3[0], l3[1], l3[2], l3[3])
```

```python
import functools
import math

import numpy as np
import jax
import jax.numpy as jnp
from jax import lax
from jax.experimental import pallas as pl
from jax.experimental.pallas import tpu as pltpu

F32 = jnp.float32
BF16 = jnp.bfloat16

RMS_EPS = 1e-6
NEG_INF = -1e30
ROPE_THETA = 10000.0

MLA_HEADS = 8
MLA_NOPE = 128
MLA_ROPE = 64
MLA_Q_RANK = 384
MLA_KV_RANK = 256
DA_HEAD_DIM = 64
DA_HEADS = 8
DA_KV_HEADS = 4
DL_CONFIGS = ((128, 1), (512, 4), (2048, 16))
DL_HEADS = 8
DL_HEAD_DIM = 128
N_BACK = 128

LANES = 128
VMEM_LIMIT = 48 * 1024 * 1024


def _cparams(sem):
    return pltpu.CompilerParams(dimension_semantics=sem, vmem_limit_bytes=VMEM_LIMIT)


def _dot(a, b):
    return jnp.dot(a, b, preferred_element_type=F32)


def _dot_t(a, b):
    return lax.dot_general(a, b, (((1,), (1,)), ((), ())), preferred_element_type=F32)


def _norm_mod(x, g, shift, scale):
    ms = jnp.mean(x * x, axis=-1, keepdims=True)
    y = x * lax.rsqrt(ms + RMS_EPS) * g
    return y * (1.0 + scale) + shift


def _rms_scale(sumsq, n):
    return lax.rsqrt(sumsq * (1.0 / n) + RMS_EPS)


def _tok_tiles(x, prompt_tile):
    nb, r, _ = x.shape
    if r >= prompt_tile:
        return 1, prompt_tile
    return min(nb, 128), r


def _adaln_body(c_ref, w_ref, b_ref, o_ref):
    c = c_ref[...]
    a = (c * jax.nn.sigmoid(c)).astype(BF16)
    o_ref[0] = _dot(a, w_ref[...].astype(BF16)) + b_ref[0]


def _adaln(c, w, b):
    rows, d = c.shape
    n = w.shape[1] // d
    return pl.pallas_call(
        _adaln_body,
        grid=(n,),
        in_specs=[pl.BlockSpec((rows, d), lambda j: (0, 0)),
                  pl.BlockSpec((d, d), lambda j: (0, j)),
                  pl.BlockSpec((1, 1, d), lambda j: (j, 0, 0))],
        out_specs=pl.BlockSpec((1, rows, d), lambda j: (j, 0, 0)),
        out_shape=jax.ShapeDtypeStruct((n, rows, d), F32),
        compiler_params=_cparams(("arbitrary",)),
        name="adaln",
    )(c, w, b.reshape(n, 1, d))


def _group_proj_body(x_ref, g_ref, sh_ref, sc_ref, w_ref, gn_ref, o_ref, h_scr, *, group, tile_kind):
    bb, tt, d = x_ref.shape
    rows = bb * tt
    tn = w_ref.shape[1]
    j = pl.program_id(2)

    @pl.when(j == 0)
    def _():
        h = _norm_mod(x_ref[...], g_ref[...], sh_ref[...], sc_ref[...])
        h_scr[...] = h.reshape(rows, d).astype(BF16)

    y = _dot(h_scr[...], w_ref[...])
    raw = tile_kind(j)

    @pl.when(raw)
    def _():
        o_ref[...] = y.reshape(bb, tt, tn)

    @pl.when(jnp.logical_not(raw))
    def _():
        gn = gn_ref[0]
        lane = lax.broadcasted_iota(jnp.int32, (rows, LANES), 1)
        parts = []
        for c in range(tn // LANES):
            yc = y[:, LANES * c:LANES * (c + 1)]
            sq = yc * yc
            if group == LANES:
                r = _rms_scale(jnp.sum(sq, axis=-1, keepdims=True), group)
            else:
                lo = jnp.sum(jnp.where(lane < group, sq, 0.0), axis=-1, keepdims=True)
                hi = jnp.sum(jnp.where(lane >= group, sq, 0.0), axis=-1, keepdims=True)
                r = jnp.where(lane < group, _rms_scale(lo, group), _rms_scale(hi, group))
            parts.append(yc * r)
        yn = jnp.concatenate(parts, axis=-1) * gn
        o_ref[...] = yn.reshape(bb, tt, tn)


def _group_proj(x, g, shift, scale, w_bf, gains, gain_idx, tile_kind, group, tn, prompt_tile):
    nb, r, d = x.shape
    n = w_bf.shape[1]
    bb, tt = _tok_tiles(x, prompt_tile)
    body = functools.partial(_group_proj_body, group=group, tile_kind=tile_kind)
    return pl.pallas_call(
        body,
        grid=(nb // bb, r // tt, n // tn),
        in_specs=[pl.BlockSpec((bb, tt, d), lambda b, t, j: (b, t, 0)),
                  pl.BlockSpec((1, 1, d), lambda b, t, j: (0, 0, 0)),
                  pl.BlockSpec((bb, 1, d), lambda b, t, j: (b, 0, 0)),
                  pl.BlockSpec((bb, 1, d), lambda b, t, j: (b, 0, 0)),
                  pl.BlockSpec((d, tn), lambda b, t, j: (0, j)),
                  pl.BlockSpec((1, 1, tn), lambda b, t, j: (gain_idx(j), 0, 0))],
        out_specs=pl.BlockSpec((bb, tt, tn), lambda b, t, j: (b, t, j)),
        out_shape=jax.ShapeDtypeStruct((nb, r, n), F32),
        scratch_shapes=[pltpu.VMEM((bb * tt, d), BF16)],
        compiler_params=_cparams(("parallel", "parallel", "arbitrary")),
        name="group_proj",
    )(x, g.reshape(1, 1, d), shift, scale, w_bf, gains)


def _mla_proj_body(x_ref, g_ref, sh_ref, sc_ref, cos_ref, sin_ref, wd_ref, gq_ref, gkv_ref,
                   wuq_ref, wuk_ref, wuv_ref, qn_ref, qp_ref, kn_ref, kp_ref, *outs, sample):
    bb, tt, d = x_ref.shape
    rows = bb * tt
    nh, nope = MLA_HEADS, MLA_NOPE
    dk = nope + MLA_ROPE
    if sample:
        ckv_ref, kpe_ref, a_ref, ape_ref = outs
    else:
        ckv_ref, kpe_ref, q_ref, k_ref, v_ref = outs

    def rope(x, rot):
        x3 = x.reshape(bb, tt, LANES) * cos_ref[...] + rot.reshape(bb, tt, LANES) * sin_ref[...]
        return x3.reshape(rows, LANES)

    h = _norm_mod(x_ref[...], g_ref[...], sh_ref[...], sc_ref[...]).reshape(rows, d).astype(BF16)
    down = _dot(h, wd_ref[...])
    cq = down[:, :MLA_Q_RANK]
    ckv = down[:, MLA_Q_RANK:MLA_Q_RANK + MLA_KV_RANK]
    o = MLA_Q_RANK + MLA_KV_RANK
    kpe = rope(down[:, o:o + LANES], down[:, o + LANES:o + 2 * LANES])
    cq = (cq * _rms_scale(jnp.sum(cq * cq, axis=-1, keepdims=True), MLA_Q_RANK) * gq_ref[...]).astype(BF16)
    ckv = ckv * _rms_scale(jnp.sum(ckv * ckv, axis=-1, keepdims=True), MLA_KV_RANK) * gkv_ref[...]
    ckv_ref[...] = ckv.reshape(bb, tt, MLA_KV_RANK)
    kpe_ref[...] = kpe[:, :MLA_ROPE].reshape(bb, tt, MLA_ROPE)
    ckv_b = ckv.astype(BF16)

    qa = _dot(cq, wuq_ref[...])
    for hh in range(nh):
        qn = qa[:, LANES * hh:LANES * (hh + 1)]
        qp = rope(qa[:, LANES * (nh + hh):LANES * (nh + hh + 1)],
                  qa[:, LANES * (2 * nh + hh):LANES * (2 * nh + hh + 1)])
        r = _rms_scale(jnp.sum(qn * qn, axis=-1, keepdims=True)
                       + jnp.sum(qp * qp, axis=-1, keepdims=True), dk)
        qn = qn * r * qn_ref[...]
        qp = qp * r * qp_ref[...]
        if sample:
            qk = (qn * kn_ref[...]).astype(BF16)
            a = _dot(qk, wuk_ref[hh])
            a_ref[:, :, 2 * LANES * hh:2 * LANES * (hh + 1)] = a.reshape(bb, tt, 2 * LANES)
            ape_ref[:, :, LANES * hh:LANES * (hh + 1)] = (qp * kp_ref[...]).reshape(bb, tt, LANES)
        else:
            q_ref[:, :, 2 * LANES * hh:2 * LANES * hh + LANES] = qn.reshape(bb, tt, LANES).astype(q_ref.dtype)
            q_ref[:, :, 2 * LANES * hh + LANES:2 * LANES * (hh + 1)] = qp.reshape(bb, tt, LANES).astype(q_ref.dtype)

    if not sample:
        kn_all = _dot(ckv_b, wuk_ref[...])
        pe_sq = jnp.sum(kpe * kpe, axis=-1, keepdims=True)
        for hh in range(nh):
            kn = kn_all[:, LANES * hh:LANES * (hh + 1)]
            r = _rms_scale(jnp.sum(kn * kn, axis=-1, keepdims=True) + pe_sq, dk)
            k_ref[:, :, 2 * LANES * hh:2 * LANES * hh + LANES] = (kn * r * kn_ref[...]).reshape(bb, tt, LANES).astype(k_ref.dtype)
            k_ref[:, :, 2 * LANES * hh + LANES:2 * LANES * (hh + 1)] = (kpe * r * kp_ref[...]).reshape(bb, tt, LANES).astype(k_ref.dtype)
        v_ref[...] = _dot(ckv_b, wuv_ref[...]).reshape(bb, tt, nh * LANES).astype(v_ref.dtype)


def _mla_weights(w_dqkv, w_uq, w_uk, w_uv, q_norm, k_norm):
    d = w_dqkv.shape[0]
    half = MLA_ROPE // 2
    nh = MLA_HEADS

    def rot_cols(w):
        return jnp.concatenate([-w[..., half:], w[..., :half]], axis=-1)

    def pad_lanes(w):
        return jnp.pad(w, [(0, 0)] * (w.ndim - 1) + [(0, LANES - w.shape[-1])])

    o = MLA_Q_RANK + MLA_KV_RANK
    w_pe = w_dqkv[:, o:]
    wd = jnp.concatenate([w_dqkv[:, :o], pad_lanes(w_pe), pad_lanes(rot_cols(w_pe))], axis=1).astype(BF16)
    uq_n = w_uq[:, :, :MLA_NOPE].reshape(MLA_Q_RANK, nh * MLA_NOPE)
    uq_p = w_uq[:, :, MLA_NOPE:]
    wuq = jnp.concatenate([uq_n, pad_lanes(uq_p).reshape(MLA_Q_RANK, nh * LANES),
                           pad_lanes(rot_cols(uq_p)).reshape(MLA_Q_RANK, nh * LANES)], axis=1).astype(BF16)

    def gain_pe(gn):
        pe = gn[MLA_NOPE:]
        return pad_lanes(jnp.concatenate([pe, pe])).reshape(1, LANES)

    scale = (MLA_NOPE + MLA_ROPE) ** -0.5
    gains = (q_norm[:MLA_NOPE].reshape(1, MLA_NOPE) * scale, gain_pe(q_norm) * scale,
             k_norm[:MLA_NOPE].reshape(1, MLA_NOPE), gain_pe(k_norm))
    wuk2 = w_uk.reshape(MLA_KV_RANK, nh * MLA_NOPE).astype(BF16)
    wuk_t = jnp.transpose(w_uk, (1, 2, 0)).astype(BF16)
    wuv2 = w_uv.reshape(MLA_KV_RANK, nh * MLA_NOPE).astype(BF16)
    return wd, wuq, wuk2, wuk_t, wuv2, gains


def _rope_tables(pos):
    half = MLA_ROPE // 2
    freqs = jnp.power(ROPE_THETA, -jnp.arange(half, dtype=F32) / half)
    ang = pos.astype(F32)[:, None] * freqs[None, :]
    cos = jnp.tile(jnp.cos(ang), (1, LANES // half))
    sin = jnp.tile(jnp.sin(ang), (1, LANES // half))
    return cos[None], sin[None]


def _mla_proj(x, g, shift, scale, pos, wts, g_q, g_kv, sample, prompt_tile):
    nb, r, d = x.shape
    wd, wuq, wuk2, wuk_t, wuv2, gains = wts
    bb, tt = _tok_tiles(x, prompt_tile)
    cos, sin = _rope_tables(pos)
    nh = MLA_HEADS
    wuk = wuk_t if sample else wuk2
    const = lambda a: pl.BlockSpec(a.shape, lambda b, t: (0,) * a.ndim)
    tok = lambda n: pl.BlockSpec((bb, tt, n), lambda b, t: (b, t, 0))
    mod = pl.BlockSpec((bb, 1, d), lambda b, t: (b, 0, 0))
    tab = pl.BlockSpec((1, tt, LANES), lambda b, t: (0, t, 0))
    gq = g_q.reshape(1, MLA_Q_RANK)
    gkv = g_kv.reshape(1, MLA_KV_RANK)
    g3 = g.reshape(1, 1, d)
    if sample:
        outs = [(MLA_KV_RANK, F32), (MLA_ROPE, F32), (nh * 2 * LANES, F32), (nh * LANES, F32)]
    else:
        outs = [(MLA_KV_RANK, F32), (MLA_ROPE, F32), (nh * 2 * LANES, BF16), (nh * 2 * LANES, BF16), (nh * LANES, BF16)]
    return pl.pallas_call(
        functools.partial(_mla_proj_body, sample=sample),
        grid=(nb // bb, r // tt),
        in_specs=[tok(d), const(g3), mod, mod, tab, tab, const(wd), const(gq), const(gkv),
                  const(wuq), const(wuk), const(wuv2)] + [const(a) for a in gains],
        out_specs=[tok(n) for n, _ in outs],
        out_shape=[jax.ShapeDtypeStruct((nb, r, n), dt) for n, dt in outs],
        compiler_params=_cparams(("parallel", "parallel")),
        name="mla_proj_sample" if sample else "mla_proj",
    )(x, g3, shift, scale, cos, sin, wd, gq, gkv, wuq, wuk, wuv2, *gains)


def _flash_body(q_ref, k_ref, v_ref, sl_ref, o_ref, qs_scr, m_scr, acc_scr, *, n_stack, alibi, split_maps, chunk):
    tq = q_ref.shape[1]
    tk = k_ref.shape[1]
    dv = v_ref.shape[2]
    rows = n_stack * tq
    i = pl.program_id(2)
    j = pl.program_id(3)
    last = ((i + 1) * tq - 1) // tk
    first_diag = (i * tq) // tk

    @pl.when(j == 0)
    def _():
        m_scr[...] = jnp.full(m_scr.shape, NEG_INF, F32)
        acc_scr[...] = jnp.zeros(acc_scr.shape, F32)
        q = q_ref[0]
        if split_maps:
            lane = lax.broadcasted_iota(jnp.int32, (tq, LANES), 1)
            c = 0
            for hh in range(q.shape[1] // LANES):
                qh = q[:, LANES * hh:LANES * (hh + 1)]
                for mp in range(2):
                    keep = (lane < DA_HEAD_DIM) if mp == 0 else (lane >= DA_HEAD_DIM)
                    qs_scr[c * tq:(c + 1) * tq, 0:LANES] = jnp.where(keep, qh, 0.0).astype(BF16)
                    c += 1
            qs_scr[:, LANES:2 * LANES] = sl_ref[0].astype(BF16)
        else:
            qs_scr[...] = q.astype(BF16)

    def step(masked):
        k = k_ref[0].astype(BF16)
        v = v_ref[0].astype(BF16)
        if alibi:
            pos = j * tk + lax.broadcasted_iota(jnp.int32, (tk, LANES), 0)
            lane = lax.broadcasted_iota(jnp.int32, (tk, LANES), 1)
            feat = jnp.where(lane == 0, pos // LANES, jnp.where(lane == 1, pos % LANES, 0))
            k = jnp.concatenate([k, feat.astype(F32).astype(BF16)], axis=-1)
        vx = jnp.concatenate([v, jnp.ones_like(v)], axis=-1)
        nt = tk // LANES
        for c in range(rows // chunk):
            rs = slice(c * chunk, (c + 1) * chunk)
            s = _dot_t(qs_scr[rs, :], k)
            if masked:
                kpos = j * tk + lax.broadcasted_iota(jnp.int32, (chunk, tk), 1)
                qpos = i * tq + (c * chunk) % tq + lax.broadcasted_iota(jnp.int32, (chunk, tk), 0)
                s = jnp.where(kpos <= qpos, s, NEG_INF)
            cols = [s[:, LANES * u:LANES * (u + 1)] for u in range(nt)]
            mx = cols[0]
            for u in range(1, nt):
                mx = jnp.maximum(mx, cols[u])
            m_old = m_scr[rs, :]
            m_new = jnp.maximum(m_old, jnp.broadcast_to(jnp.max(mx, axis=-1, keepdims=True), (chunk, LANES)))
            p = jnp.concatenate([jnp.exp(cu - m_new) for cu in cols], axis=-1).astype(BF16)
            corr = jnp.exp(m_old - m_new)
            acc_scr[rs, :] = acc_scr[rs, :] * jnp.concatenate([corr, corr], axis=-1) + _dot(p, vx)
            m_scr[rs, :] = m_new

    @pl.when(j < first_diag)
    def _():
        step(False)

    @pl.when((j >= first_diag) & (j <= last))
    def _():
        step(True)

    @pl.when(j == pl.num_programs(3) - 1)
    def _():
        acc = acc_scr[...]
        o = acc[:, :dv] / acc[:, dv:]
        for c in range(n_stack):
            o_ref[0, :, dv * c:dv * (c + 1)] = o[c * tq:(c + 1) * tq].astype(o_ref.dtype)


def _flash(q_arr, k_arr, v_arr, slopes, *, n_groups, n_stack, dq, dk, dv, q_col, k_col, v_col,
           tq, tk, alibi, split_maps, out_dtype):
    b, t, _ = q_arr.shape
    tq, tk = min(tq, t), min(tk, t)
    nq, nk = t // tq, t // tk
    rows = n_stack * tq
    chunk = min(FLASH_ROW_CHUNK, rows)
    assert chunk <= tq and tq % chunk == 0

    def kv_idx(col):
        def f(bi, g, i, j):
            return (bi, jnp.minimum(j, ((i + 1) * tq - 1) // tk), col + g)
        return f

    body = functools.partial(_flash_body, n_stack=n_stack, alibi=alibi, split_maps=split_maps, chunk=chunk)
    return pl.pallas_call(
        body,
        grid=(b, n_groups, nq, nk),
        in_specs=[pl.BlockSpec((1, tq, dq), lambda bi, g, i, j: (bi, i, q_col + g)),
                  pl.BlockSpec((1, tk, dk), kv_idx(k_col)),
                  pl.BlockSpec((1, tk, dv), kv_idx(v_col)),
                  pl.BlockSpec((1, rows, LANES), lambda bi, g, i, j: (g, 0, 0))],
        out_specs=pl.BlockSpec((1, tq, n_stack * dv), lambda bi, g, i, j: (bi, i, g)),
        out_shape=jax.ShapeDtypeStruct((b, t, n_groups * n_stack * dv), out_dtype),
        scratch_shapes=[pltpu.VMEM((rows, dk + (LANES if alibi else 0)), BF16), pltpu.VMEM((rows, LANES), F32),
                        pltpu.VMEM((rows, 2 * dv), F32)],
        compiler_params=_cparams(("parallel", "parallel", "parallel", "arbitrary")),
        name="flash",
    )(q_arr, k_arr, v_arr, slopes)


def _band_body(*refs, tile):
    n_g = len(DL_CONFIGS)
    sl_ref = refs[0]
    grp = [refs[1 + 5 * g:1 + 5 * (g + 1)] for g in range(n_g)]
    o_ref = refs[1 + 5 * n_g]
    m_scr, l_scr, acc_scr = refs[2 + 5 * n_g:]
    i = pl.program_id(1)
    nb = N_BACK
    slope = sl_ref[0]
    a_i = lax.broadcasted_iota(jnp.int32, (nb, 2 * nb), 0)
    b_i = lax.broadcasted_iota(jnp.int32, (nb, 2 * nb), 1)
    steps = a_i - b_i + nb
    band = (steps >= 0) & (steps <= nb)
    band_first = band & (b_i >= jnp.where(i == 0, nb, 0))

    for g, (win, dil) in enumerate(DL_CONFIGS):
        q_ref, kc_ref, vc_ref, kp_ref, vp_ref = grp[g]
        n_sub = tile // dil
        bias = -slope[:, :1] * (steps * dil).astype(F32)
        for r in range(dil):
            for qb in range(n_sub // nb):
                sel = pl.ds(r + dil * nb * qb, nb, stride=dil) if dil > 1 else pl.ds(nb * qb, nb)
                q = q_ref[0, sel, :].astype(BF16)
                if qb > 0:
                    psel = pl.ds(r + dil * nb * (qb - 1), nb, stride=dil) if dil > 1 else pl.ds(nb * (qb - 1), nb)
                    k_prev, v_prev = kc_ref[0, psel, :], vc_ref[0, psel, :]
                else:
                    psel = pl.ds(r, nb, stride=dil) if dil > 1 else pl.ds(0, nb)
                    k_prev, v_prev = kp_ref[0, psel, :], vp_ref[0, psel, :]
                k = jnp.concatenate([k_prev, kc_ref[0, sel, :]], axis=0).astype(BF16)
                v = jnp.concatenate([v_prev, vc_ref[0, sel, :]], axis=0).astype(BF16)
                s = _dot_t(q, k) + bias
                s = jnp.where(band_first if qb == 0 else band, s, NEG_INF)
                m_blk = jnp.max(s, axis=-1, keepdims=True)
                if g == 0:
                    m_new = jnp.broadcast_to(m_blk, (nb, LANES))
                    p = jnp.exp(s - m_blk)
                    l_new = jnp.broadcast_to(jnp.sum(p, axis=-1, keepdims=True), (nb, LANES))
                    acc_new = _dot(p.astype(BF16), v)
                else:
                    m_old = m_scr[sel, :]
                    m_new = jnp.maximum(m_old, m_blk)
                    corr = jnp.exp(m_old - m_new)
                    p = jnp.exp(s - m_new[:, :1])
                    l_new = l_scr[sel, :] * corr + jnp.sum(p, axis=-1, keepdims=True)
                    acc_new = acc_scr[sel, :] * corr + _dot(p.astype(BF16), v)
                m_scr[sel, :] = m_new
                l_scr[sel, :] = l_new
                acc_scr[sel, :] = acc_new
    o_ref[0] = (acc_scr[...] / l_scr[...]).astype(o_ref.dtype)


def _band_attention(qkv, slopes_lane, tile, out_dtype):
    b, t, _ = qkv.shape
    nh = DL_HEADS
    tile = min(tile, t)
    in_specs = [pl.BlockSpec((1, 1, LANES), lambda bi, i, h: (h, 0, 0))]
    args = [slopes_lane]
    for g, (win, dil) in enumerate(DL_CONFIGS):
        halo = N_BACK * dil
        per = tile // halo
        col = 3 * nh * g
        in_specs += [
            pl.BlockSpec((1, tile, LANES), lambda bi, i, h, c=col: (bi, i, c + h)),
            pl.BlockSpec((1, tile, LANES), lambda bi, i, h, c=col: (bi, i, c + nh + h)),
            pl.BlockSpec((1, tile, LANES), lambda bi, i, h, c=col: (bi, i, c + 2 * nh + h)),
            pl.BlockSpec((1, halo, LANES), lambda bi, i, h, c=col, p=per: (bi, jnp.maximum(i * p - 1, 0), c + nh + h)),
            pl.BlockSpec((1, halo, LANES), lambda bi, i, h, c=col, p=per: (bi, jnp.maximum(i * p - 1, 0), c + 2 * nh + h)),
        ]
        args += [qkv] * 5
    return pl.pallas_call(
        functools.partial(_band_body, tile=tile),
        grid=(b, t // tile, nh),
        in_specs=in_specs,
        out_specs=pl.BlockSpec((1, tile, LANES), lambda bi, i, h: (bi, i, h)),
        out_shape=jax.ShapeDtypeStruct((b, t, nh * LANES), out_dtype),
        scratch_shapes=[pltpu.VMEM((tile, LANES), F32)] * 3,
        compiler_params=_cparams(("parallel", "parallel", "parallel")),
        name="band_attention",
    )(*args)


def _out_proj_body(*refs, mode, lam_init):
    if mode == "plain":
        a_ref, wo_ref, x_ref, gate_ref, o_ref = refs
    elif mode == "diff":
        a_ref, lq1, lk1, lq2, lk2, sub_ref, wo_ref, x_ref, gate_ref, o_ref = refs
    else:
        o0, o1, o2, e0, e1, e2, wo_ref, x_ref, gate_ref, o_ref = refs
    bb, tt, d = x_ref.shape
    rows = bb * tt
    if mode == "plain":
        a = a_ref[...].reshape(rows, a_ref.shape[2]).astype(BF16)
    elif mode == "diff":
        lam = (jnp.exp(jnp.sum(lq1[...] * lk1[...], axis=-1, keepdims=True))
               - jnp.exp(jnp.sum(lq2[...] * lk2[...], axis=-1, keepdims=True)) + lam_init)
        o = a_ref[...].reshape(rows, a_ref.shape[2]).astype(F32)
        parts = []
        for hh in range(DA_HEADS):
            dlt = o[:, 2 * LANES * hh:2 * LANES * hh + LANES] - lam * o[:, 2 * LANES * hh + LANES:2 * LANES * (hh + 1)]
            r = _rms_scale(jnp.sum(dlt * dlt, axis=-1, keepdims=True), LANES)
            parts.append(dlt * r * sub_ref[...] * (1.0 - lam_init))
        a = jnp.concatenate(parts, axis=-1).astype(BF16)
    else:
        e = [e0[...], e1[...], e2[...]]
        mx = jnp.maximum(jnp.maximum(e[0], e[1]), e[2])
        w = [jnp.exp(ei - mx) for ei in e]
        den = w[0] + w[1] + w[2]
        a = (w[0] * o0[...] + w[1] * o1[...] + w[2] * o2[...]) / den
        a = a.reshape(rows, d).astype(BF16)
    y = _dot(a, wo_ref[...])
    o_ref[...] = x_ref[...] + gate_ref[...] * y.reshape(bb, tt, d)


def _out_proj(acts, extras, wo_bf, x, gate, mode, lam_init, prompt_tile):
    nb, r, d = x.shape
    bb, tt = _tok_tiles(x, prompt_tile)
    tok = lambda n: pl.BlockSpec((bb, tt, n), lambda b, t: (b, t, 0))
    const = lambda a: pl.BlockSpec(a.shape, lambda b, t: (0,) * a.ndim)
    in_specs = [tok(a.shape[2]) for a in acts] + [const(e) for e in extras] + [
        const(wo_bf), tok(d), pl.BlockSpec((bb, 1, d), lambda b, t: (b, 0, 0))]
    return pl.pallas_call(
        functools.partial(_out_proj_body, mode=mode, lam_init=lam_init),
        grid=(nb // bb, r // tt),
        in_specs=in_specs,
        out_specs=tok(d),
        out_shape=jax.ShapeDtypeStruct((nb, r, d), F32),
        compiler_params=_cparams(("parallel", "parallel")),
        name="out_proj_" + mode,
    )(*acts, *extras, wo_bf, x, gate)


def _ffn_body(x_ref, g_ref, sh_ref, sc_ref, gate_ref, w1_ref, w3_ref, w2_ref, o_ref, h_scr, acc_scr):
    bb, tt, d = x_ref.shape
    rows = bb * tt
    j = pl.program_id(2)

    @pl.when(j == 0)
    def _():
        h = _norm_mod(x_ref[...], g_ref[...], sh_ref[...], sc_ref[...])
        h_scr[...] = h.reshape(rows, d).astype(BF16)
        acc_scr[...] = jnp.zeros(acc_scr.shape, F32)

    h = h_scr[...]
    a = _dot(h, w1_ref[...])
    b = _dot(h, w3_ref[...])
    u = (a * jax.nn.sigmoid(a) * b).astype(BF16)
    acc_scr[...] += _dot(u, w2_ref[...])

    @pl.when(j == pl.num_programs(2) - 1)
    def _():
        o_ref[...] = x_ref[...] + gate_ref[...] * acc_scr[...].reshape(bb, tt, d)


def _ffn(x, g, shift, scale, gate, w1, w3, w2, th, prompt_tile):
    nb, r, d = x.shape
    hid = w1.shape[1]
    bb, tt = _tok_tiles(x, prompt_tile)
    mod = pl.BlockSpec((bb, 1, d), lambda b, t, j: (b, 0, 0))
    return pl.pallas_call(
        _ffn_body,
        grid=(nb // bb, r // tt, hid // th),
        in_specs=[pl.BlockSpec((bb, tt, d), lambda b, t, j: (b, t, 0)),
                  pl.BlockSpec((1, 1, d), lambda b, t, j: (0, 0, 0)),
                  mod, mod, mod,
                  pl.BlockSpec((d, th), lambda b, t, j: (0, j)),
                  pl.BlockSpec((d, th), lambda b, t, j: (0, j)),
                  pl.BlockSpec((th, d), lambda b, t, j: (j, 0))],
        out_specs=pl.BlockSpec((bb, tt, d), lambda b, t, j: (b, t, 0)),
        out_shape=jax.ShapeDtypeStruct((nb, r, d), F32),
        scratch_shapes=[pltpu.VMEM((bb * tt, d), BF16), pltpu.VMEM((bb * tt, d), F32)],
        compiler_params=_cparams(("parallel", "parallel", "arbitrary")),
        name="ffn",
    )(x, g.reshape(1, 1, d), shift, scale, gate, w1, w3, w2)


def _softmax_update(s, m_scr, l_scr):
    m_old = m_scr[...]
    m_new = jnp.maximum(m_old, jnp.max(s, axis=-1, keepdims=True))
    corr = jnp.exp(m_old - m_new)
    p = jnp.exp(s - m_new)
    l_scr[...] = l_scr[...] * corr + jnp.sum(p, axis=-1, keepdims=True)
    m_scr[...] = m_new
    return p, corr


def _mla_decode_body(pt_ref, a_ref, ape_ref, ckv_ref, kpet_ref, wuk_ref, wuv_ref, *rest, pages, per):
    lat_refs = rest[:pages]
    pet_refs = rest[pages:2 * pages]
    o_ref = rest[2 * pages]
    aq_scr, apq_scr, m_scr, l_scr, acc_scr, latn_scr = rest[2 * pages + 1:]
    nh = MLA_HEADS
    s_tok = a_ref.shape[1]
    nrow = nh * s_tok
    c = pl.program_id(1)
    dk = MLA_NOPE + MLA_ROPE

    @pl.when(c == 0)
    def _():
        m_scr[...] = jnp.full(m_scr.shape, NEG_INF, F32)
        l_scr[...] = jnp.zeros(l_scr.shape, F32)
        acc_scr[...] = jnp.zeros(acc_scr.shape, F32)
        for hh in range(nh):
            aq_scr[s_tok * hh:s_tok * (hh + 1), :] = a_ref[0, :, 2 * LANES * hh:2 * LANES * (hh + 1)].astype(BF16)
            apq_scr[s_tok * hh:s_tok * (hh + 1), :] = ape_ref[0, :, LANES * hh:LANES * hh + MLA_ROPE].astype(BF16)

    def scores(latb, pet):
        nk = latb.shape[0]
        kt = _dot_t(wuk_ref[...], latb)
        pe_sq = jnp.sum(pet * pet, axis=0, keepdims=True)
        ssq = []
        for hh in range(nh):
            kh = kt[MLA_NOPE * hh:MLA_NOPE * (hh + 1), :]
            ssq.append(jnp.broadcast_to(jnp.sum(kh * kh, axis=0, keepdims=True) + pe_sq, (s_tok, nk)))
        r = _rms_scale(jnp.concatenate(ssq, axis=0), dk)
        return (_dot_t(aq_scr[...], latb) + _dot(apq_scr[...], pet.astype(BF16))) * r

    def absorb(s, latb):
        p, corr = _softmax_update(s, m_scr, l_scr)
        acc_scr[...] = acc_scr[...] * corr + _dot(p.astype(BF16), latb)

    n_sub = pages // per
    latb = [jnp.concatenate([lat_refs[u * per + w][0] for w in range(per)], axis=0).astype(BF16) for u in range(n_sub)]
    pet = [jnp.concatenate([pet_refs[u * per + w][0] for w in range(per)], axis=1) for u in range(n_sub)]
    absorb(jnp.concatenate([scores(latb[u], pet[u]) for u in range(n_sub)], axis=1), jnp.concatenate(latb, axis=0))

    @pl.when(c == pl.num_programs(1) - 1)
    def _():
        latn_scr[...] = jnp.zeros(latn_scr.shape, F32)
        latn_scr[0:s_tok, :] = ckv_ref[0]
        qi = lax.broadcasted_iota(jnp.int32, (nrow, LANES), 0) % s_tok
        kj = lax.broadcasted_iota(jnp.int32, (nrow, LANES), 1)
        latn = latn_scr[...].astype(BF16)
        absorb(jnp.where(kj <= qi, scores(latn, kpet_ref[0]), NEG_INF), latn)
        o_lat = (acc_scr[...] / l_scr[...]).astype(BF16)
        for hh in range(nh):
            o_ref[0, :, LANES * hh:LANES * (hh + 1)] = _dot(
                o_lat[s_tok * hh:s_tok * (hh + 1), :], wuv_ref[:, LANES * hh:LANES * (hh + 1)])


def _mla_decode(a, ape, ckv_s, kpe_s, cache_lat, cache_kpe, page_table, wuk_rows, wuv2, pages, per):
    db, s_tok, _ = a.shape
    n_pages = page_table.shape[1]
    page = cache_lat.shape[1]
    pages = min(pages, n_pages)
    nh = MLA_HEADS
    pt = page_table.reshape(-1)
    cache_pet = jnp.swapaxes(cache_kpe, 1, 2)
    kpet_s = jnp.pad(jnp.swapaxes(kpe_s, 1, 2), ((0, 0), (0, 0), (0, LANES - s_tok)))

    def page_spec(shape, jj):
        return pl.BlockSpec((1,) + shape, lambda b, c, pt_ref: (pt_ref[b * n_pages + c * pages + jj], 0, 0))

    seq = lambda arr: pl.BlockSpec((1,) + arr.shape[1:], lambda b, c, pt_ref: (b, 0, 0))
    const = lambda arr: pl.BlockSpec(arr.shape, lambda b, c, pt_ref: (0,) * arr.ndim)
    grid_spec = pltpu.PrefetchScalarGridSpec(
        num_scalar_prefetch=1,
        grid=(db, n_pages // pages),
        in_specs=[seq(a), seq(ape), seq(ckv_s), seq(kpet_s), const(wuk_rows), const(wuv2)]
                 + [page_spec((page, MLA_KV_RANK), jj) for jj in range(pages)]
                 + [page_spec((MLA_ROPE, page), jj) for jj in range(pages)],
        out_specs=pl.BlockSpec((1, s_tok, nh * LANES), lambda b, c, pt_ref: (b, 0, 0)),
        scratch_shapes=[pltpu.VMEM((nh * s_tok, MLA_KV_RANK), BF16), pltpu.VMEM((nh * s_tok, MLA_ROPE), BF16),
                        pltpu.VMEM((nh * s_tok, 1), F32), pltpu.VMEM((nh * s_tok, 1), F32),
                        pltpu.VMEM((nh * s_tok, MLA_KV_RANK), F32),
                        pltpu.VMEM((LANES, MLA_KV_RANK), F32)],
    )
    return pl.pallas_call(
        functools.partial(_mla_decode_body, pages=pages, per=per),
        grid_spec=grid_spec,
        out_shape=jax.ShapeDtypeStruct((db, s_tok, nh * LANES), F32),
        compiler_params=_cparams(("parallel", "arbitrary")),
        name="mla_decode",
    )(pt, a, ape, ckv_s, kpet_s, wuk_rows, wuv2, *([cache_lat] * pages), *([cache_pet] * pages))


def _diff_decode_body(pt_ref, qkv_ref, sl_ref, *rest, pages, past):
    kt_refs = rest[:pages]
    v_refs = rest[pages:2 * pages]
    o_ref = rest[2 * pages]
    qz_scr, m_scr, l_scr, acc_scr, kn_scr, vn_scr = rest[2 * pages + 1:]
    s_tok = qkv_ref.shape[1]
    nkv = DA_KV_HEADS
    per_kv = (DA_HEADS // nkv) * 2 * s_tok
    nrow = nkv * per_kv
    page = kt_refs[0].shape[2]
    c = pl.program_id(1)
    nq = DA_HEADS * 2 * DA_HEAD_DIM

    @pl.when(c == 0)
    def _():
        m_scr[...] = jnp.full(m_scr.shape, NEG_INF, F32)
        l_scr[...] = jnp.zeros(l_scr.shape, F32)
        acc_scr[...] = jnp.zeros(acc_scr.shape, F32)
        lane = lax.broadcasted_iota(jnp.int32, (s_tok, LANES), 1)
        row = 0
        for hh in range(DA_HEADS):
            qh = qkv_ref[0, :, LANES * hh:LANES * (hh + 1)]
            for mp in range(2):
                keep = (lane < DA_HEAD_DIM) if mp == 0 else (lane >= DA_HEAD_DIM)
                qz_scr[row:row + s_tok, :] = jnp.where(keep, qh, 0.0).astype(BF16)
                row += s_tok

    def absorb(s, vs, kpos0, mask):
        nk = s.shape[1]
        rel = kpos0 + lax.broadcasted_iota(jnp.int32, (1, nk), 1)
        s = s + sl_ref[...] * rel.astype(F32)
        if mask is not None:
            s = jnp.where(mask, s, NEG_INF)
        p, corr = _softmax_update(s, m_scr, l_scr)
        pb = p.astype(BF16)
        pv = jnp.concatenate([_dot(pb[per_kv * kv:per_kv * (kv + 1), :], vs[kv]) for kv in range(nkv)], axis=0)
        acc_scr[...] = acc_scr[...] * corr + pv

    def qz(kv):
        return qz_scr[per_kv * kv:per_kv * (kv + 1), :]

    kt = jnp.concatenate([kt_refs[w][0] for w in range(pages)], axis=1).astype(BF16)
    s = jnp.concatenate([_dot(qz(kv), kt[LANES * kv:LANES * (kv + 1), :]) for kv in range(nkv)], axis=0)
    vs = [jnp.concatenate([v_refs[w][0, pl.ds(kv, page, stride=nkv), :] for w in range(pages)], axis=0).astype(BF16)
          for kv in range(nkv)]
    absorb(s, vs, c * pages * page - past, None)

    @pl.when(c == pl.num_programs(1) - 1)
    def _():
        kn_scr[...] = jnp.zeros(kn_scr.shape, F32)
        vn_scr[...] = jnp.zeros(vn_scr.shape, F32)
        kn_scr[0:s_tok, :] = qkv_ref[0, :, nq:nq + nkv * LANES]
        vn_scr[0:s_tok, :] = qkv_ref[0, :, nq + nkv * LANES:nq + 2 * nkv * LANES]
        qi = lax.broadcasted_iota(jnp.int32, (nrow, LANES), 0) % s_tok
        kj = lax.broadcasted_iota(jnp.int32, (nrow, LANES), 1)
        kn = kn_scr[...].astype(BF16)
        vn = vn_scr[...].astype(BF16)
        s = jnp.concatenate([_dot_t(qz(kv), kn[:, LANES * kv:LANES * (kv + 1)]) for kv in range(nkv)], axis=0)
        absorb(s, [vn[:, LANES * kv:LANES * (kv + 1)] for kv in range(nkv)], 0, kj <= qi)
        o_ref[0] = acc_scr[...] / l_scr[...]


def _diff_decode(qkv_s, cache_k, cache_v, page_table, slopes_rows, pages):
    db, s_tok, _ = qkv_s.shape
    n_pages = page_table.shape[1]
    n_pool, page = cache_k.shape[:2]
    pages = min(pages, n_pages)
    width = DA_KV_HEADS * LANES
    nrow = DA_HEADS * 2 * s_tok
    pt = page_table.reshape(-1)
    cache_kt = jnp.transpose(cache_k, (0, 2, 3, 4, 1)).reshape(n_pool, width, page)
    cache_v2 = cache_v.reshape(n_pool, page * DA_KV_HEADS, LANES)

    def kt_spec(jj):
        return pl.BlockSpec((1, width, page), lambda b, c, pt_ref: (pt_ref[b * n_pages + c * pages + jj], 0, 0))

    def v_spec(jj):
        return pl.BlockSpec((1, page * DA_KV_HEADS, LANES), lambda b, c, pt_ref: (pt_ref[b * n_pages + c * pages + jj], 0, 0))

    grid_spec = pltpu.PrefetchScalarGridSpec(
        num_scalar_prefetch=1,
        grid=(db, n_pages // pages),
        in_specs=[pl.BlockSpec((1, s_tok, qkv_s.shape[2]), lambda b, c, pt_ref: (b, 0, 0)),
                  pl.BlockSpec((nrow, 1), lambda b, c, pt_ref: (0, 0))]
                 + [kt_spec(jj) for jj in range(pages)] + [v_spec(jj) for jj in range(pages)],
        out_specs=pl.BlockSpec((1, nrow, LANES), lambda b, c, pt_ref: (b, 0, 0)),
        scratch_shapes=[pltpu.VMEM((nrow, LANES), BF16), pltpu.VMEM((nrow, 1), F32), pltpu.VMEM((nrow, 1), F32),
                        pltpu.VMEM((nrow, LANES), F32),
                        pltpu.VMEM((LANES, width), F32), pltpu.VMEM((LANES, width), F32)],
    )
    return pl.pallas_call(
        functools.partial(_diff_decode_body, pages=pages, past=n_pages * page),
        grid_spec=grid_spec,
        out_shape=jax.ShapeDtypeStruct((db, nrow, LANES), F32),
        compiler_params=_cparams(("parallel", "arbitrary")),
        name="diff_decode",
    )(pt, qkv_s, slopes_rows, *([cache_kt] * pages), *([cache_v2] * pages))


def _dil_decode_body(qkv_ref, sl_ref, buf_ref, o_ref, lse_ref, roll_ref, kn_scr, vn_scr, new_scr, sem, *, dil, n_res):
    s_tok = qkv_ref.shape[1]
    nh = DL_HEADS
    hd = nh * DL_HEAD_DIM
    nrow = nh * s_tok
    per_entry = 2 * nh
    rows = buf_ref.shape[1]
    nb = rows // (per_entry * dil)
    b = pl.program_id(0)

    keep = rows - s_tok * per_entry
    shift = pltpu.make_async_copy(buf_ref.at[0, pl.ds(s_tok * per_entry, keep)], roll_ref.at[b, pl.ds(0, keep)], sem.at[0])
    shift.start()
    for kv in range(2):
        for hh in range(nh):
            lo = hd * (1 + kv) + DL_HEAD_DIM * hh
            new_scr[pl.ds(kv * nh + hh, s_tok, stride=per_entry), :] = qkv_ref[0, :, lo:lo + DL_HEAD_DIM]
    append = pltpu.make_async_copy(new_scr, roll_ref.at[b, pl.ds(keep, s_tok * per_entry)], sem.at[1])
    append.start()

    q = qkv_ref[0, :, :hd]
    row_h = lax.broadcasted_iota(jnp.int32, (nrow, hd), 0) // s_tok
    col_h = lax.broadcasted_iota(jnp.int32, (nrow, hd), 1) // DL_HEAD_DIM
    qbd = jnp.where(row_h == col_h, jnp.concatenate([q] * nh, axis=0), 0.0).astype(BF16)
    kn_scr[...] = jnp.zeros(kn_scr.shape, F32)
    vn_scr[...] = jnp.zeros(vn_scr.shape, F32)
    kn_scr[0:s_tok, :] = qkv_ref[0, :, hd:2 * hd]
    vn_scr[0:s_tok, :] = qkv_ref[0, :, 2 * hd:3 * hd]

    qi = lax.broadcasted_iota(jnp.int32, (nrow, nb), 0) % s_tok
    kj = lax.broadcasted_iota(jnp.int32, (nrow, nb), 1)
    q_res, q_t = qi % dil, qi // dil
    slope = sl_ref[...]

    def gather(r, kv):
        return jnp.concatenate(
            [buf_ref[0, pl.ds((r * 2 + kv) * nh + hh, nb, stride=per_entry * dil), :] for hh in range(nh)],
            axis=-1).astype(BF16)

    blocks = []
    for r in range(n_res):
        steps = nb + q_t - kj
        s = _dot_t(qbd, gather(r, 0)) - slope * (steps * dil).astype(F32)
        blocks.append((jnp.where((q_res == r) & (kj >= q_t), s, NEG_INF), gather(r, 1)))
    s = _dot_t(qbd, kn_scr[...].astype(BF16)) - slope * (qi - kj).astype(F32)
    ok = (kj < s_tok) & (kj <= qi) & (kj % dil == q_res)
    blocks.append((jnp.where(ok, s, NEG_INF), vn_scr[...].astype(BF16)))

    m = blocks[0][0].max(axis=-1, keepdims=True)
    for s, _ in blocks[1:]:
        m = jnp.maximum(m, s.max(axis=-1, keepdims=True))
    l = jnp.zeros((nrow, 1), F32)
    acc = jnp.zeros((nrow, hd), F32)
    for s, v in blocks:
        p = jnp.exp(s - m)
        l = l + jnp.sum(p, axis=-1, keepdims=True)
        acc = acc + _dot(p.astype(BF16), v)
    out = acc / l
    lse = m + jnp.log(l)
    for hh in range(nh):
        rs = slice(s_tok * hh, s_tok * (hh + 1))
        cs = slice(DL_HEAD_DIM * hh, DL_HEAD_DIM * (hh + 1))
        o_ref[0, :, cs] = out[rs, cs]
        lse_ref[0, :, cs] = jnp.broadcast_to(lse[rs], (s_tok, DL_HEAD_DIM))
    shift.wait()
    append.wait()


def _dil_decode(qkv_s, buf, slopes_rows, g, dil):
    db, s_tok, _ = qkv_s.shape
    nh = DL_HEADS
    hd = nh * DL_HEAD_DIM
    wb = buf.shape[1]
    assert wb == N_BACK * dil, "window buffer must hold the whole window"
    n_res = min(dil, s_tok)
    rows = wb * 2 * nh
    buf2 = buf.reshape(db, rows, DL_HEAD_DIM)
    out = jax.ShapeDtypeStruct((db, s_tok, hd), F32)
    o_g, lse_g, rolled = pl.pallas_call(
        functools.partial(_dil_decode_body, dil=dil, n_res=n_res),
        grid=(db,),
        in_specs=[pl.BlockSpec((1, s_tok, 3 * hd), lambda b: (b, 0, g)),
                  pl.BlockSpec((nh * s_tok, 1), lambda b: (0, 0)),
                  pl.BlockSpec((1, rows, DL_HEAD_DIM), lambda b: (b, 0, 0))],
        out_specs=[pl.BlockSpec((1, s_tok, hd), lambda b: (b, 0, 0))] * 2 + [pl.BlockSpec(memory_space=pl.ANY)],
        out_shape=[out, out, jax.ShapeDtypeStruct(buf2.shape, buf.dtype)],
        scratch_shapes=[pltpu.VMEM((LANES, hd), F32), pltpu.VMEM((LANES, hd), F32),
                        pltpu.VMEM((s_tok * 2 * nh, DL_HEAD_DIM), F32), pltpu.SemaphoreType.DMA((2,))],
        compiler_params=_cparams(("arbitrary",)),
        name="dil_decode",
    )(qkv_s, slopes_rows, buf2)
    return o_g, lse_g, rolled.reshape(buf.shape)


PROMPT_TILE = 512
FFN_TILE = 1024
FFN_HIDDEN_TILE = 256
DECODE_PAGES = 8
MLA_DECODE_PAGES = 16
DECODE_SUB_PAGES = 2
FLASH_ROW_CHUNK = 256


def _alibi_slopes(n):
    return np.array([2.0 ** (-8.0 * (i + 1) / n) for i in range(n)], dtype=np.float32)


def _mla_layer(hp_args, hs_args, caches, page_table, params):
    w_dqkv, g_q, g_kv, w_uq, w_uk, w_uv, q_norm, k_norm, w_o = params
    xp, gmix, shp, scp = hp_args
    xs, _, shs, scs = hs_args
    cache_lat, cache_kpe = caches
    wts = _mla_weights(w_dqkv, w_uq, w_uk, w_uv, q_norm, k_norm)
    b, t, _ = xp.shape
    db, s_tok, _ = xs.shape
    past = page_table.shape[1] * cache_lat.shape[1]
    ckv_p, kpe_p, q, k, v = _mla_proj(xp, gmix, shp, scp, jnp.arange(t), wts, g_q, g_kv, False, PROMPT_TILE)
    nh = MLA_HEADS
    o_p = _flash(q, k, v, jnp.zeros((nh, min(2048, t), LANES), F32), n_groups=nh, n_stack=1,
                 dq=2 * LANES, dk=2 * LANES, dv=LANES, q_col=0, k_col=0, v_col=0,
                 tq=2048, tk=1024, alibi=False, split_maps=False, out_dtype=BF16)
    ckv_s, kpe_s, a, ape = _mla_proj(xs, gmix, shs, scs, past + jnp.arange(s_tok), wts, g_q, g_kv, True, PROMPT_TILE)
    wuk_rows = wts[3].reshape(nh * MLA_NOPE, MLA_KV_RANK)
    o_s = _mla_decode(a, ape, ckv_s, kpe_s, cache_lat, cache_kpe, page_table, wuk_rows, wts[4],
                      MLA_DECODE_PAGES, DECODE_SUB_PAGES)
    return (o_p,), (o_s,), "plain", (), w_o, (ckv_p, ckv_s, kpe_p, kpe_s)


def _diff_layer(hp_args, hs_args, caches, page_table, layer_idx, params):
    w_qkv, q_norm, k_norm, lq1, lk1, lq2, lk2, subln, w_o = params
    xp, gmix, shp, scp = hp_args
    xs, _, shs, scs = hs_args
    cache_k, cache_v = caches
    b, t, _ = xp.shape
    db, s_tok, _ = xs.shape
    scale = DA_HEAD_DIM ** -0.5
    tn = 512
    rep = tn // DA_HEAD_DIM
    gains = jnp.stack([jnp.tile(q_norm, rep) * scale, jnp.tile(k_norm, rep), jnp.ones((tn,), F32)]).reshape(3, 1, tn)
    nq_tiles = DA_HEADS * 2 * DA_HEAD_DIM // tn
    nk_tiles = DA_KV_HEADS * 2 * DA_HEAD_DIM // tn
    gain_idx = lambda j: jnp.maximum(j - (nq_tiles - 1), 0)
    kind = lambda j: j >= nq_tiles + nk_tiles
    wq = w_qkv.astype(BF16)
    qkv_p = _group_proj(xp, gmix, shp, scp, wq, gains, gain_idx, kind, DA_HEAD_DIM, tn, FFN_TILE)
    qkv_s = _group_proj(xs, gmix, shs, scs, wq, gains, gain_idx, kind, DA_HEAD_DIM, tn, PROMPT_TILE)
    slopes = _alibi_slopes(DA_HEADS)
    tq = min(512, t)
    per_kv = DA_HEADS // DA_KV_HEADS
    sl_rows = np.repeat(slopes.reshape(DA_KV_HEADS, per_kv), 2 * tq, axis=1).reshape(DA_KV_HEADS, per_kv * 2 * tq, 1)
    sl_feat = np.zeros(sl_rows.shape[:2] + (LANES,), np.float32)
    sl_feat[:, :, 0:1] = sl_rows * LANES
    sl_feat[:, :, 1:2] = sl_rows
    assert np.array_equal(sl_feat, sl_feat.astype(BF16).astype(np.float32)), "slopes must be exact in bf16"
    kc = DA_HEADS * 2 * DA_HEAD_DIM // LANES
    o_p = _flash(qkv_p, qkv_p, qkv_p, jnp.asarray(sl_feat), n_groups=DA_KV_HEADS, n_stack=per_kv * 2,
                 dq=per_kv * LANES, dk=LANES, dv=LANES, q_col=0, k_col=kc, v_col=kc + DA_KV_HEADS,
                 tq=tq, tk=1024, alibi=True, split_maps=True, out_dtype=BF16)
    sl_dec = jnp.asarray(np.repeat(slopes, 2 * s_tok).reshape(DA_HEADS * 2 * s_tok, 1))
    o_dec = _diff_decode(qkv_s, cache_k, cache_v, page_table, sl_dec, DECODE_PAGES)
    o_s = jnp.transpose(o_dec.reshape(db, DA_HEADS * 2, s_tok, LANES), (0, 2, 1, 3)).reshape(db, s_tok, DA_HEADS * 2 * LANES)
    nq = DA_HEADS * 2 * DA_HEAD_DIM
    nk = DA_KV_HEADS * 2 * DA_HEAD_DIM
    k_p = qkv_p[..., nq:nq + nk].reshape(b, t, DA_KV_HEADS, 2, DA_HEAD_DIM)
    k_s = qkv_s[..., nq:nq + nk].reshape(db, s_tok, DA_KV_HEADS, 2, DA_HEAD_DIM)
    v_p = qkv_p[..., nq + nk:].reshape(b, t, DA_KV_HEADS, 2 * DA_HEAD_DIM)
    v_s = qkv_s[..., nq + nk:].reshape(db, s_tok, DA_KV_HEADS, 2 * DA_HEAD_DIM)
    extras = tuple(a.reshape(1, -1) for a in (lq1, lk1, lq2, lk2, subln))
    return (o_p,), (o_s,), "diff", extras, w_o, (k_p, k_s, v_p, v_s)


def _dilated_layer(hp_args, hs_args, caches, params):
    w_qkv, q_norm, k_norm, w_o = params
    xp, gmix, shp, scp = hp_args
    xs, _, shs, scs = hs_args
    b, t, _ = xp.shape
    db, s_tok, _ = xs.shape
    nh, hd = DL_HEADS, DL_HEAD_DIM
    scale = hd ** -0.5
    tn = nh * hd
    gains = jnp.stack([jnp.tile(q_norm, nh) * scale, jnp.tile(k_norm, nh), jnp.ones((tn,), F32)]).reshape(3, 1, tn)
    wq = w_qkv.astype(BF16)
    gain_idx = lambda j: j % 3
    kind = lambda j: j % 3 == 2
    qkv_p = _group_proj(xp, gmix, shp, scp, wq, gains, gain_idx, kind, hd, tn, FFN_TILE)
    qkv_s = _group_proj(xs, gmix, shs, scs, wq, gains, gain_idx, kind, hd, tn, PROMPT_TILE)
    slopes = _alibi_slopes(nh)
    sl_lane = jnp.asarray(np.repeat(slopes, LANES).reshape(nh, 1, LANES))
    o_p = _band_attention(qkv_p, sl_lane, 2048, BF16)
    sl_rows = jnp.asarray(np.repeat(slopes, s_tok).reshape(nh * s_tok, 1))
    outs, lses, st = [], [], []
    qkv_p6 = qkv_p.reshape(b, t, len(DL_CONFIGS), 3, nh, hd)
    for g, (win, dil) in enumerate(DL_CONFIGS):
        o_g, lse_g, rolled = _dil_decode(qkv_s, caches[g], sl_rows, g, dil)
        outs.append(o_g)
        lses.append(lse_g)
        wb = min(win, t)
        st.append(qkv_p6[:, t - wb:, g, 1:3])
        st.append(rolled)
    return (o_p,), tuple(outs) + tuple(lses), ("plain", "dil"), (), w_o, tuple(st)


def kernel(x_prompt, x_sample, cache_l0_latent, cache_l0_kpe, cache_l1_k, cache_l1_v, cache_l2_kv_w128, cache_l2_kv_w512, cache_l2_kv_w2048, cache_l3_latent, cache_l3_kpe, page_table, c_prompt, c_sample, l0_ada_w, l0_ada_b, l0_norm_mix, l0_w_dqkv, l0_g_q, l0_g_kv, l0_w_uq, l0_w_uk, l0_w_uv, l0_q_norm, l0_k_norm, l0_w_o, l0_norm_ffn, l0_ffn_w1, l0_ffn_w3, l0_ffn_w2, l1_ada_w, l1_ada_b, l1_norm_mix, l1_w_qkv, l1_q_norm, l1_k_norm, l1_lambda_q1, l1_lambda_k1, l1_lambda_q2, l1_lambda_k2, l1_subln, l1_w_o, l1_norm_ffn, l1_ffn_w1, l1_ffn_w3, l1_ffn_w2, l2_ada_w, l2_ada_b, l2_norm_mix, l2_w_qkv, l2_q_norm, l2_k_norm, l2_w_o, l2_norm_ffn, l2_ffn_w1, l2_ffn_w3, l2_ffn_w2, l3_ada_w, l3_ada_b, l3_norm_mix, l3_w_dqkv, l3_g_q, l3_g_kv, l3_w_uq, l3_w_uk, l3_w_uv, l3_q_norm, l3_k_norm, l3_w_o, l3_norm_ffn, l3_ffn_w1, l3_ffn_w3, l3_ffn_w2):
    block = [
        (l0_ada_w, l0_ada_b, l0_norm_mix, l0_norm_ffn, l0_ffn_w1, l0_ffn_w3, l0_ffn_w2),
        (l1_ada_w, l1_ada_b, l1_norm_mix, l1_norm_ffn, l1_ffn_w1, l1_ffn_w3, l1_ffn_w2),
        (l2_ada_w, l2_ada_b, l2_norm_mix, l2_norm_ffn, l2_ffn_w1, l2_ffn_w3, l2_ffn_w2),
        (l3_ada_w, l3_ada_b, l3_norm_mix, l3_norm_ffn, l3_ffn_w1, l3_ffn_w3, l3_ffn_w2),
    ]
    mixer = [
        (l0_w_dqkv, l0_g_q, l0_g_kv, l0_w_uq, l0_w_uk, l0_w_uv, l0_q_norm, l0_k_norm, l0_w_o),
        (l1_w_qkv, l1_q_norm, l1_k_norm, l1_lambda_q1, l1_lambda_k1, l1_lambda_q2, l1_lambda_k2, l1_subln, l1_w_o),
        (l2_w_qkv, l2_q_norm, l2_k_norm, l2_w_o),
        (l3_w_dqkv, l3_g_q, l3_g_kv, l3_w_uq, l3_w_uk, l3_w_uv, l3_q_norm, l3_k_norm, l3_w_o),
    ]
    caches = [
        (cache_l0_latent, cache_l0_kpe),
        (cache_l1_k, cache_l1_v),
        (cache_l2_kv_w128, cache_l2_kv_w512, cache_l2_kv_w2048),
        (cache_l3_latent, cache_l3_kpe),
    ]
    b, t, d = x_prompt.shape
    db, s_tok, _ = x_sample.shape
    rows = b + db
    rows_pad = -(-rows // 8) * 8
    c_all = jnp.pad(jnp.concatenate([c_prompt, c_sample], axis=0), ((0, rows_pad - rows), (0, 0)))
    xp, xs = x_prompt, x_sample
    states = []
    for i in range(len(block)):
        ada_w, ada_b, g_mix, g_ffn, w1, w3, w2 = block[i]
        mod = _adaln(c_all, ada_w, ada_b)
        mod_p = mod[:, :b].reshape(6, b, 1, d)
        mod_s = mod[:, b:rows].reshape(6, db, 1, d)
        hp_args = (xp, g_mix, mod_p[0], mod_p[1])
        hs_args = (xs, g_mix, mod_s[0], mod_s[1])
        kind = i % 3
        lam_init = 0.0
        if kind == 0:
            a_p, a_s, mode, extras, w_o, st = _mla_layer(hp_args, hs_args, caches[i], page_table, mixer[i])
        elif kind == 1:
            lam_init = 0.8 - 0.6 * math.exp(-0.3 * i)
            a_p, a_s, mode, extras, w_o, st = _diff_layer(hp_args, hs_args, caches[i], page_table, i, mixer[i])
        else:
            a_p, a_s, mode, extras, w_o, st = _dilated_layer(hp_args, hs_args, caches[i], mixer[i])
        mode_p, mode_s = mode if isinstance(mode, tuple) else (mode, mode)
        wo_bf = w_o.astype(BF16)
        xp = _out_proj(a_p, extras, wo_bf, xp, mod_p[2], mode_p, lam_init, PROMPT_TILE)
        xs = _out_proj(a_s, extras, wo_bf, xs, mod_s[2], mode_s, lam_init, PROMPT_TILE)
        w1b, w3b, w2b = w1.astype(BF16), w3.astype(BF16), w2.astype(BF16)
        xp = _ffn(xp, g_ffn, mod_p[3], mod_p[4], mod_p[5], w1b, w3b, w2b, FFN_HIDDEN_TILE, FFN_TILE)
        xs = _ffn(xs, g_ffn, mod_s[3], mod_s[4], mod_s[5], w1b, w3b, w2b, FFN_HIDDEN_TILE, FFN_TILE)
        states.append(st)
    l0, l1, l2, l3 = states
    return (xp, xs, l0[0], l0[1], l0[2], l0[3], l1[0], l1[1], l1[2], l1[3],
            l2[0], l2[1], l2[2], l2[3], l2[4], l2[5], l3[0], l3[1], l3[2], l3[3])
```

```python
import functools
import math

import numpy as np
import jax
import jax.numpy as jnp
from jax import lax
from jax.experimental import pallas as pl
from jax.experimental.pallas import tpu as pltpu

F32 = jnp.float32
BF16 = jnp.bfloat16

RMS_EPS = 1e-6
NEG_INF = -1e30
ROPE_THETA = 10000.0

MLA_HEADS = 8
MLA_NOPE = 128
MLA_ROPE = 64
MLA_Q_RANK = 384
MLA_KV_RANK = 256
DA_HEAD_DIM = 64
DA_HEADS = 8
DA_KV_HEADS = 4
DL_CONFIGS = ((128, 1), (512, 4), (2048, 16))
DL_HEADS = 8
DL_HEAD_DIM = 128
N_BACK = 128

LANES = 128
VMEM_LIMIT = 48 * 1024 * 1024


def _cparams(sem):
    return pltpu.CompilerParams(dimension_semantics=sem, vmem_limit_bytes=VMEM_LIMIT)


def _dot(a, b):
    return jnp.dot(a, b, preferred_element_type=F32)


def _dot_t(a, b):
    return lax.dot_general(a, b, (((1,), (1,)), ((), ())), preferred_element_type=F32)


def _norm_mod(x, g, shift, scale):
    ms = jnp.mean(x * x, axis=-1, keepdims=True)
    y = x * lax.rsqrt(ms + RMS_EPS) * g
    return y * (1.0 + scale) + shift


def _rms_scale(sumsq, n):
    return lax.rsqrt(sumsq * (1.0 / n) + RMS_EPS)


def _tok_tiles(x, prompt_tile):
    nb, r, _ = x.shape
    if r >= prompt_tile:
        return 1, prompt_tile
    return min(nb, 128), r


def _adaln_body(c_ref, w_ref, b_ref, o_ref):
    c = c_ref[...]
    a = (c * jax.nn.sigmoid(c)).astype(BF16)
    o_ref[0] = _dot(a, w_ref[...].astype(BF16)) + b_ref[0]


def _adaln(c, w, b):
    rows, d = c.shape
    n = w.shape[1] // d
    return pl.pallas_call(
        _adaln_body,
        grid=(n,),
        in_specs=[pl.BlockSpec((rows, d), lambda j: (0, 0)),
                  pl.BlockSpec((d, d), lambda j: (0, j)),
                  pl.BlockSpec((1, 1, d), lambda j: (j, 0, 0))],
        out_specs=pl.BlockSpec((1, rows, d), lambda j: (j, 0, 0)),
        out_shape=jax.ShapeDtypeStruct((n, rows, d), F32),
        compiler_params=_cparams(("arbitrary",)),
        name="adaln",
    )(c, w, b.reshape(n, 1, d))


def _group_proj_body(x_ref, g_ref, sh_ref, sc_ref, w_ref, gn_ref, o_ref, h_scr, *, group, tile_kind):
    bb, tt, d = x_ref.shape
    rows = bb * tt
    tn = w_ref.shape[1]
    j = pl.program_id(2)

    @pl.when(j == 0)
    def _():
        h = _norm_mod(x_ref[...], g_ref[...], sh_ref[...], sc_ref[...])
        h_scr[...] = h.reshape(rows, d).astype(BF16)

    y = _dot(h_scr[...], w_ref[...])
    raw = tile_kind(j)

    @pl.when(raw)
    def _():
        o_ref[...] = y.reshape(bb, tt, tn)

    @pl.when(jnp.logical_not(raw))
    def _():
        gn = gn_ref[0]
        lane = lax.broadcasted_iota(jnp.int32, (rows, LANES), 1)
        parts = []
        for c in range(tn // LANES):
            yc = y[:, LANES * c:LANES * (c + 1)]
            sq = yc * yc
            if group == LANES:
                r = _rms_scale(jnp.sum(sq, axis=-1, keepdims=True), group)
            else:
                lo = jnp.sum(jnp.where(lane < group, sq, 0.0), axis=-1, keepdims=True)
                hi = jnp.sum(jnp.where(lane >= group, sq, 0.0), axis=-1, keepdims=True)
                r = jnp.where(lane < group, _rms_scale(lo, group), _rms_scale(hi, group))
            parts.append(yc * r)
        yn = jnp.concatenate(parts, axis=-1) * gn
        o_ref[...] = yn.reshape(bb, tt, tn)


def _group_proj(x, g, shift, scale, w_bf, gains, gain_idx, tile_kind, group, tn, prompt_tile):
    nb, r, d = x.shape
    n = w_bf.shape[1]
    bb, tt = _tok_tiles(x, prompt_tile)
    body = functools.partial(_group_proj_body, group=group, tile_kind=tile_kind)
    return pl.pallas_call(
        body,
        grid=(nb // bb, r // tt, n // tn),
        in_specs=[pl.BlockSpec((bb, tt, d), lambda b, t, j: (b, t, 0)),
                  pl.BlockSpec((1, 1, d), lambda b, t, j: (0, 0, 0)),
                  pl.BlockSpec((bb, 1, d), lambda b, t, j: (b, 0, 0)),
                  pl.BlockSpec((bb, 1, d), lambda b, t, j: (b, 0, 0)),
                  pl.BlockSpec((d, tn), lambda b, t, j: (0, j)),
                  pl.BlockSpec((1, 1, tn), lambda b, t, j: (gain_idx(j), 0, 0))],
        out_specs=pl.BlockSpec((bb, tt, tn), lambda b, t, j: (b, t, j)),
        out_shape=jax.ShapeDtypeStruct((nb, r, n), F32),
        scratch_shapes=[pltpu.VMEM((bb * tt, d), BF16)],
        compiler_params=_cparams(("parallel", "parallel", "arbitrary")),
        name="group_proj",
    )(x, g.reshape(1, 1, d), shift, scale, w_bf, gains)


def _mla_proj_body(x_ref, g_ref, sh_ref, sc_ref, cos_ref, sin_ref, wd_ref, gq_ref, gkv_ref,
                   wuq_ref, wuk_ref, wuv_ref, qn_ref, qp_ref, kn_ref, kp_ref, *outs, sample):
    bb, tt, d = x_ref.shape
    rows = bb * tt
    nh, nope = MLA_HEADS, MLA_NOPE
    dk = nope + MLA_ROPE
    if sample:
        ckv_ref, kpe_ref, a_ref, ape_ref = outs
    else:
        ckv_ref, kpe_ref, q_ref, k_ref, v_ref = outs

    def rope(x, rot):
        x3 = x.reshape(bb, tt, LANES) * cos_ref[...] + rot.reshape(bb, tt, LANES) * sin_ref[...]
        return x3.reshape(rows, LANES)

    h = _norm_mod(x_ref[...], g_ref[...], sh_ref[...], sc_ref[...]).reshape(rows, d).astype(BF16)
    down = _dot(h, wd_ref[...])
    cq = down[:, :MLA_Q_RANK]
    ckv = down[:, MLA_Q_RANK:MLA_Q_RANK + MLA_KV_RANK]
    o = MLA_Q_RANK + MLA_KV_RANK
    kpe = rope(down[:, o:o + LANES], down[:, o + LANES:o + 2 * LANES])
    cq = (cq * _rms_scale(jnp.sum(cq * cq, axis=-1, keepdims=True), MLA_Q_RANK) * gq_ref[...]).astype(BF16)
    ckv = ckv * _rms_scale(jnp.sum(ckv * ckv, axis=-1, keepdims=True), MLA_KV_RANK) * gkv_ref[...]
    ckv_ref[...] = ckv.reshape(bb, tt, MLA_KV_RANK)
    kpe_ref[...] = kpe[:, :MLA_ROPE].reshape(bb, tt, MLA_ROPE)
    ckv_b = ckv.astype(BF16)

    qa = _dot(cq, wuq_ref[...])
    for hh in range(nh):
        qn = qa[:, LANES * hh:LANES * (hh + 1)]
        qp = rope(qa[:, LANES * (nh + hh):LANES * (nh + hh + 1)],
                  qa[:, LANES * (2 * nh + hh):LANES * (2 * nh + hh + 1)])
        r = _rms_scale(jnp.sum(qn * qn, axis=-1, keepdims=True)
                       + jnp.sum(qp * qp, axis=-1, keepdims=True), dk)
        qn = qn * r * qn_ref[...]
        qp = qp * r * qp_ref[...]
        if sample:
            qk = (qn * kn_ref[...]).astype(BF16)
            a = _dot(qk, wuk_ref[hh])
            a_ref[:, :, 2 * LANES * hh:2 * LANES * (hh + 1)] = a.reshape(bb, tt, 2 * LANES)
            ape_ref[:, :, LANES * hh:LANES * (hh + 1)] = (qp * kp_ref[...]).reshape(bb, tt, LANES)
        else:
            q_ref[:, :, 2 * LANES * hh:2 * LANES * hh + LANES] = qn.reshape(bb, tt, LANES).astype(q_ref.dtype)
            q_ref[:, :, 2 * LANES * hh + LANES:2 * LANES * (hh + 1)] = qp.reshape(bb, tt, LANES).astype(q_ref.dtype)

    if not sample:
        kn_all = _dot(ckv_b, wuk_ref[...])
        pe_sq = jnp.sum(kpe * kpe, axis=-1, keepdims=True)
        for hh in range(nh):
            kn = kn_all[:, LANES * hh:LANES * (hh + 1)]
            r = _rms_scale(jnp.sum(kn * kn, axis=-1, keepdims=True) + pe_sq, dk)
            k_ref[:, :, 2 * LANES * hh:2 * LANES * hh + LANES] = (kn * r * kn_ref[...]).reshape(bb, tt, LANES).astype(k_ref.dtype)
            k_ref[:, :, 2 * LANES * hh + LANES:2 * LANES * (hh + 1)] = (kpe * r * kp_ref[...]).reshape(bb, tt, LANES).astype(k_ref.dtype)
        v_ref[...] = _dot(ckv_b, wuv_ref[...]).reshape(bb, tt, nh * LANES).astype(v_ref.dtype)


def _mla_weights(w_dqkv, w_uq, w_uk, w_uv, q_norm, k_norm):
    d = w_dqkv.shape[0]
    half = MLA_ROPE // 2
    nh = MLA_HEADS

    def rot_cols(w):
        return jnp.concatenate([-w[..., half:], w[..., :half]], axis=-1)

    def pad_lanes(w):
        return jnp.pad(w, [(0, 0)] * (w.ndim - 1) + [(0, LANES - w.shape[-1])])

    o = MLA_Q_RANK + MLA_KV_RANK
    w_pe = w_dqkv[:, o:]
    wd = jnp.concatenate([w_dqkv[:, :o], pad_lanes(w_pe), pad_lanes(rot_cols(w_pe))], axis=1).astype(BF16)
    uq_n = w_uq[:, :, :MLA_NOPE].reshape(MLA_Q_RANK, nh * MLA_NOPE)
    uq_p = w_uq[:, :, MLA_NOPE:]
    wuq = jnp.concatenate([uq_n, pad_lanes(uq_p).reshape(MLA_Q_RANK, nh * LANES),
                           pad_lanes(rot_cols(uq_p)).reshape(MLA_Q_RANK, nh * LANES)], axis=1).astype(BF16)

    def gain_pe(gn):
        pe = gn[MLA_NOPE:]
        return pad_lanes(jnp.concatenate([pe, pe])).reshape(1, LANES)

    scale = (MLA_NOPE + MLA_ROPE) ** -0.5
    gains = (q_norm[:MLA_NOPE].reshape(1, MLA_NOPE) * scale, gain_pe(q_norm) * scale,
             k_norm[:MLA_NOPE].reshape(1, MLA_NOPE), gain_pe(k_norm))
    wuk2 = w_uk.reshape(MLA_KV_RANK, nh * MLA_NOPE).astype(BF16)
    wuk_t = jnp.transpose(w_uk, (1, 2, 0)).astype(BF16)
    wuv2 = w_uv.reshape(MLA_KV_RANK, nh * MLA_NOPE).astype(BF16)
    return wd, wuq, wuk2, wuk_t, wuv2, gains


def _rope_tables(pos):
    half = MLA_ROPE // 2
    freqs = jnp.power(ROPE_THETA, -jnp.arange(half, dtype=F32) / half)
    ang = pos.astype(F32)[:, None] * freqs[None, :]
    cos = jnp.tile(jnp.cos(ang), (1, LANES // half))
    sin = jnp.tile(jnp.sin(ang), (1, LANES // half))
    return cos[None], sin[None]


def _mla_proj(x, g, shift, scale, pos, wts, g_q, g_kv, sample, prompt_tile):
    nb, r, d = x.shape
    wd, wuq, wuk2, wuk_t, wuv2, gains = wts
    bb, tt = _tok_tiles(x, prompt_tile)
    cos, sin = _rope_tables(pos)
    nh = MLA_HEADS
    wuk = wuk_t if sample else wuk2
    const = lambda a: pl.BlockSpec(a.shape, lambda b, t: (0,) * a.ndim)
    tok = lambda n: pl.BlockSpec((bb, tt, n), lambda b, t: (b, t, 0))
    mod = pl.BlockSpec((bb, 1, d), lambda b, t: (b, 0, 0))
    tab = pl.BlockSpec((1, tt, LANES), lambda b, t: (0, t, 0))
    gq = g_q.reshape(1, MLA_Q_RANK)
    gkv = g_kv.reshape(1, MLA_KV_RANK)
    g3 = g.reshape(1, 1, d)
    if sample:
        outs = [(MLA_KV_RANK, F32), (MLA_ROPE, F32), (nh * 2 * LANES, F32), (nh * LANES, F32)]
    else:
        outs = [(MLA_KV_RANK, F32), (MLA_ROPE, F32), (nh * 2 * LANES, BF16), (nh * 2 * LANES, BF16), (nh * LANES, BF16)]
    return pl.pallas_call(
        functools.partial(_mla_proj_body, sample=sample),
        grid=(nb // bb, r // tt),
        in_specs=[tok(d), const(g3), mod, mod, tab, tab, const(wd), const(gq), const(gkv),
                  const(wuq), const(wuk), const(wuv2)] + [const(a) for a in gains],
        out_specs=[tok(n) for n, _ in outs],
        out_shape=[jax.ShapeDtypeStruct((nb, r, n), dt) for n, dt in outs],
        compiler_params=_cparams(("parallel", "parallel")),
        name="mla_proj_sample" if sample else "mla_proj",
    )(x, g3, shift, scale, cos, sin, wd, gq, gkv, wuq, wuk, wuv2, *gains)


def _flash_body(q_ref, k_ref, v_ref, sl_ref, *rest, n_stack, alibi, split_maps, chunk, n_side, side_lead, per_tile):
    src_refs = rest[:n_side]
    o_ref = rest[n_side]
    dst_refs = rest[n_side + 1:2 * n_side + 1]
    qs_scr, m_scr, acc_scr = rest[2 * n_side + 1:2 * n_side + 4]
    tq = q_ref.shape[1]
    tk = k_ref.shape[1]
    dv = v_ref.shape[2]
    rows = n_stack * tq
    i = pl.program_id(2)
    j = pl.program_id(3)
    last = ((i + 1) * tq - 1) // tk
    first_diag = (i * tq) // tk

    def side_copies(seq):
        out = []
        for c in range(n_side):
            keep = src_refs[c].shape[1] - side_lead
            out.append(pltpu.make_async_copy(src_refs[c].at[seq, pl.ds(side_lead, keep)],
                                             dst_refs[c].at[seq, pl.ds(0, keep)], sem.at[2 * c]))
            out.append(pltpu.make_async_copy(zero_scr, dst_refs[c].at[seq, pl.ds(keep, side_lead)], sem.at[2 * c + 1]))
        return out

    if n_side:
        zero_scr, sem = rest[2 * n_side + 4:]
        n_seq = src_refs[0].shape[0]
        tile_id = (pl.program_id(0) * pl.num_programs(1) + pl.program_id(1)) * pl.num_programs(2) + i

        @pl.when(j == 0)
        def _():
            zero_scr[...] = jnp.zeros(zero_scr.shape, F32)

        for u in range(per_tile):
            seq = tile_id * per_tile + u

            @pl.when((j == u) & (seq < n_seq))
            def _():
                for cp in side_copies(seq):
                    cp.start()

            @pl.when((j == jnp.minimum(u + 1, pl.num_programs(3) - 1) if per_tile > 1 else j == pl.num_programs(3) - 1) & (seq < n_seq))
            def _():
                for cp in side_copies(seq):
                    cp.wait()

    @pl.when(j == 0)
    def _():
        m_scr[...] = jnp.full(m_scr.shape, NEG_INF, F32)
        acc_scr[...] = jnp.zeros(acc_scr.shape, F32)
        q = q_ref[0]
        if split_maps:
            lane = lax.broadcasted_iota(jnp.int32, (tq, LANES), 1)
            c = 0
            for hh in range(q.shape[1] // LANES):
                qh = q[:, LANES * hh:LANES * (hh + 1)]
                for mp in range(2):
                    keep = (lane < DA_HEAD_DIM) if mp == 0 else (lane >= DA_HEAD_DIM)
                    qs_scr[c * tq:(c + 1) * tq, 0:LANES] = jnp.where(keep, qh, 0.0).astype(BF16)
                    c += 1
            qs_scr[:, LANES:2 * LANES] = sl_ref[0].astype(BF16)
        else:
            qs_scr[...] = q.astype(BF16)

    def step(masked):
        k = k_ref[0].astype(BF16)
        v = v_ref[0].astype(BF16)
        if alibi:
            pos = j * tk + lax.broadcasted_iota(jnp.int32, (tk, LANES), 0)
            lane = lax.broadcasted_iota(jnp.int32, (tk, LANES), 1)
            feat = jnp.where(lane == 0, pos // LANES, jnp.where(lane == 1, pos % LANES, 0))
            k = jnp.concatenate([k, feat.astype(F32).astype(BF16)], axis=-1)
        vx = jnp.concatenate([v, jnp.ones_like(v)], axis=-1)
        nt = tk // LANES
        for c in range(rows // chunk):
            rs = slice(c * chunk, (c + 1) * chunk)
            s = _dot_t(qs_scr[rs, :], k)
            if masked:
                kpos = j * tk + lax.broadcasted_iota(jnp.int32, (chunk, tk), 1)
                qpos = i * tq + (c * chunk) % tq + lax.broadcasted_iota(jnp.int32, (chunk, tk), 0)
                s = jnp.where(kpos <= qpos, s, NEG_INF)
            cols = [s[:, LANES * u:LANES * (u + 1)] for u in range(nt)]
            mx = cols[0]
            for u in range(1, nt):
                mx = jnp.maximum(mx, cols[u])
            m_old = m_scr[rs, :]
            m_new = jnp.maximum(m_old, jnp.broadcast_to(jnp.max(mx, axis=-1, keepdims=True), (chunk, LANES)))
            p = jnp.concatenate([jnp.exp(cu - m_new) for cu in cols], axis=-1).astype(BF16)
            corr = jnp.exp(m_old - m_new)
            acc_scr[rs, :] = acc_scr[rs, :] * jnp.concatenate([corr, corr], axis=-1) + _dot(p, vx)
            m_scr[rs, :] = m_new

    @pl.when(j < first_diag)
    def _():
        step(False)

    @pl.when((j >= first_diag) & (j <= last))
    def _():
        step(True)

    @pl.when(j == pl.num_programs(3) - 1)
    def _():
        acc = acc_scr[...]
        o = acc[:, :dv] / acc[:, dv:]
        for c in range(n_stack):
            o_ref[0, :, dv * c:dv * (c + 1)] = o[c * tq:(c + 1) * tq].astype(o_ref.dtype)


def _flash(q_arr, k_arr, v_arr, slopes, *, n_groups, n_stack, dq, dk, dv, q_col, k_col, v_col,
           tq, tk, alibi, split_maps, out_dtype, side=(), side_lead=0):
    b, t, _ = q_arr.shape
    tq, tk = min(tq, t), min(tk, t)
    nq, nk = t // tq, t // tk
    rows = n_stack * tq
    chunk = min(FLASH_ROW_CHUNK, rows)
    assert chunk <= tq and tq % chunk == 0
    n_side = len(side)
    per_tile = -(-side[0].shape[0] // (b * n_groups * nq)) if n_side else 0
    assert per_tile <= 1 or per_tile < nk, "not enough kv steps per q tile to chain the side copies"

    def kv_idx(col):
        def f(bi, g, i, j):
            return (bi, jnp.minimum(j, ((i + 1) * tq - 1) // tk), col + g)
        return f

    body = functools.partial(_flash_body, n_stack=n_stack, alibi=alibi, split_maps=split_maps, chunk=chunk,
                             n_side=n_side, side_lead=side_lead, per_tile=per_tile)
    any_spec = pl.BlockSpec(memory_space=pl.ANY)
    side_scratch = [pltpu.VMEM((side_lead, LANES), F32), pltpu.SemaphoreType.DMA((2 * n_side,))] if n_side else []
    outs = pl.pallas_call(
        body,
        grid=(b, n_groups, nq, nk),
        in_specs=[pl.BlockSpec((1, tq, dq), lambda bi, g, i, j: (bi, i, q_col + g)),
                  pl.BlockSpec((1, tk, dk), kv_idx(k_col)),
                  pl.BlockSpec((1, tk, dv), kv_idx(v_col)),
                  pl.BlockSpec((1, rows, LANES), lambda bi, g, i, j: (g, 0, 0))] + [any_spec] * n_side,
        out_specs=[pl.BlockSpec((1, tq, n_stack * dv), lambda bi, g, i, j: (bi, i, g))] + [any_spec] * n_side,
        out_shape=[jax.ShapeDtypeStruct((b, t, n_groups * n_stack * dv), out_dtype)]
                  + [jax.ShapeDtypeStruct(a.shape, a.dtype) for a in side],
        scratch_shapes=[pltpu.VMEM((rows, dk + (LANES if alibi else 0)), BF16), pltpu.VMEM((rows, LANES), F32),
                        pltpu.VMEM((rows, 2 * dv), F32)] + side_scratch,
        compiler_params=_cparams(("parallel", "parallel", "parallel", "arbitrary")),
        name="flash",
    )(q_arr, k_arr, v_arr, slopes, *side)
    return (outs[0], tuple(outs[1:])) if n_side else outs[0]


def _band_body(*refs, tile):
    n_g = len(DL_CONFIGS)
    sl_ref = refs[0]
    grp = [refs[1 + 5 * g:1 + 5 * (g + 1)] for g in range(n_g)]
    o_ref = refs[1 + 5 * n_g]
    m_scr, l_scr, acc_scr = refs[2 + 5 * n_g:]
    i = pl.program_id(1)
    nb = N_BACK
    slope = sl_ref[0]
    a_i = lax.broadcasted_iota(jnp.int32, (nb, 2 * nb), 0)
    b_i = lax.broadcasted_iota(jnp.int32, (nb, 2 * nb), 1)
    steps = a_i - b_i + nb
    band = (steps >= 0) & (steps <= nb)
    band_first = band & (b_i >= jnp.where(i == 0, nb, 0))

    for g, (win, dil) in enumerate(DL_CONFIGS):
        q_ref, kc_ref, vc_ref, kp_ref, vp_ref = grp[g]
        n_sub = tile // dil
        bias = -slope[:, :1] * (steps * dil).astype(F32)
        for r in range(dil):
            for qb in range(n_sub // nb):
                sel = pl.ds(r + dil * nb * qb, nb, stride=dil) if dil > 1 else pl.ds(nb * qb, nb)
                q = q_ref[0, sel, :].astype(BF16)
                if qb > 0:
                    psel = pl.ds(r + dil * nb * (qb - 1), nb, stride=dil) if dil > 1 else pl.ds(nb * (qb - 1), nb)
                    k_prev, v_prev = kc_ref[0, psel, :], vc_ref[0, psel, :]
                else:
                    psel = pl.ds(r, nb, stride=dil) if dil > 1 else pl.ds(0, nb)
                    k_prev, v_prev = kp_ref[0, psel, :], vp_ref[0, psel, :]
                k = jnp.concatenate([k_prev, kc_ref[0, sel, :]], axis=0).astype(BF16)
                v = jnp.concatenate([v_prev, vc_ref[0, sel, :]], axis=0).astype(BF16)
                s = _dot_t(q, k) + bias
                s = jnp.where(band_first if qb == 0 else band, s, NEG_INF)
                m_blk = jnp.max(s, axis=-1, keepdims=True)
                if g == 0:
                    m_new = jnp.broadcast_to(m_blk, (nb, LANES))
                    p = jnp.exp(s - m_blk)
                    l_new = jnp.broadcast_to(jnp.sum(p, axis=-1, keepdims=True), (nb, LANES))
                    acc_new = _dot(p.astype(BF16), v)
                else:
                    m_old = m_scr[sel, :]
                    m_new = jnp.maximum(m_old, m_blk)
                    corr = jnp.exp(m_old - m_new)
                    p = jnp.exp(s - m_new[:, :1])
                    l_new = l_scr[sel, :] * corr + jnp.sum(p, axis=-1, keepdims=True)
                    acc_new = acc_scr[sel, :] * corr + _dot(p.astype(BF16), v)
                m_scr[sel, :] = m_new
                l_scr[sel, :] = l_new
                acc_scr[sel, :] = acc_new
    o_ref[0] = (acc_scr[...] / l_scr[...]).astype(o_ref.dtype)


def _band_attention(qkv, slopes_lane, tile, out_dtype):
    b, t, _ = qkv.shape
    nh = DL_HEADS
    tile = min(tile, t)
    in_specs = [pl.BlockSpec((1, 1, LANES), lambda bi, i, h: (h, 0, 0))]
    args = [slopes_lane]
    for g, (win, dil) in enumerate(DL_CONFIGS):
        halo = N_BACK * dil
        per = tile // halo
        col = 3 * nh * g
        in_specs += [
            pl.BlockSpec((1, tile, LANES), lambda bi, i, h, c=col: (bi, i, c + h)),
            pl.BlockSpec((1, tile, LANES), lambda bi, i, h, c=col: (bi, i, c + nh + h)),
            pl.BlockSpec((1, tile, LANES), lambda bi, i, h, c=col: (bi, i, c + 2 * nh + h)),
            pl.BlockSpec((1, halo, LANES), lambda bi, i, h, c=col, p=per: (bi, jnp.maximum(i * p - 1, 0), c + nh + h)),
            pl.BlockSpec((1, halo, LANES), lambda bi, i, h, c=col, p=per: (bi, jnp.maximum(i * p - 1, 0), c + 2 * nh + h)),
        ]
        args += [qkv] * 5
    return pl.pallas_call(
        functools.partial(_band_body, tile=tile),
        grid=(b, t // tile, nh),
        in_specs=in_specs,
        out_specs=pl.BlockSpec((1, tile, LANES), lambda bi, i, h: (bi, i, h)),
        out_shape=jax.ShapeDtypeStruct((b, t, nh * LANES), out_dtype),
        scratch_shapes=[pltpu.VMEM((tile, LANES), F32)] * 3,
        compiler_params=_cparams(("parallel", "parallel", "parallel")),
        name="band_attention",
    )(*args)


def _out_proj_body(*refs, mode, lam_init):
    if mode == "plain":
        a_ref, wo_ref, x_ref, gate_ref, o_ref = refs
    elif mode == "diff":
        a_ref, lq1, lk1, lq2, lk2, sub_ref, wo_ref, x_ref, gate_ref, o_ref = refs
    else:
        o0, o1, o2, e0, e1, e2, wo_ref, x_ref, gate_ref, o_ref = refs
    bb, tt, d = x_ref.shape
    rows = bb * tt
    if mode == "plain":
        a = a_ref[...].reshape(rows, a_ref.shape[2]).astype(BF16)
    elif mode == "diff":
        lam = (jnp.exp(jnp.sum(lq1[...] * lk1[...], axis=-1, keepdims=True))
               - jnp.exp(jnp.sum(lq2[...] * lk2[...], axis=-1, keepdims=True)) + lam_init)
        o = a_ref[...].reshape(rows, a_ref.shape[2]).astype(F32)
        parts = []
        for hh in range(DA_HEADS):
            dlt = o[:, 2 * LANES * hh:2 * LANES * hh + LANES] - lam * o[:, 2 * LANES * hh + LANES:2 * LANES * (hh + 1)]
            r = _rms_scale(jnp.sum(dlt * dlt, axis=-1, keepdims=True), LANES)
            parts.append(dlt * r * sub_ref[...] * (1.0 - lam_init))
        a = jnp.concatenate(parts, axis=-1).astype(BF16)
    else:
        e = [e0[...], e1[...], e2[...]]
        mx = jnp.maximum(jnp.maximum(e[0], e[1]), e[2])
        w = [jnp.exp(ei - mx) for ei in e]
        den = w[0] + w[1] + w[2]
        a = (w[0] * o0[...] + w[1] * o1[...] + w[2] * o2[...]) / den
        a = a.reshape(rows, d).astype(BF16)
    y = _dot(a, wo_ref[...])
    o_ref[...] = x_ref[...] + gate_ref[...] * y.reshape(bb, tt, d)


def _out_proj(acts, extras, wo_bf, x, gate, mode, lam_init, prompt_tile):
    nb, r, d = x.shape
    bb, tt = _tok_tiles(x, prompt_tile)
    tok = lambda n: pl.BlockSpec((bb, tt, n), lambda b, t: (b, t, 0))
    const = lambda a: pl.BlockSpec(a.shape, lambda b, t: (0,) * a.ndim)
    in_specs = [tok(a.shape[2]) for a in acts] + [const(e) for e in extras] + [
        const(wo_bf), tok(d), pl.BlockSpec((bb, 1, d), lambda b, t: (b, 0, 0))]
    return pl.pallas_call(
        functools.partial(_out_proj_body, mode=mode, lam_init=lam_init),
        grid=(nb // bb, r // tt),
        in_specs=in_specs,
        out_specs=tok(d),
        out_shape=jax.ShapeDtypeStruct((nb, r, d), F32),
        compiler_params=_cparams(("parallel", "parallel")),
        name="out_proj_" + mode,
    )(*acts, *extras, wo_bf, x, gate)


def _ffn_body(x_ref, g_ref, sh_ref, sc_ref, gate_ref, w1_ref, w3_ref, w2_ref, o_ref, h_scr, acc_scr):
    bb, tt, d = x_ref.shape
    rows = bb * tt
    j = pl.program_id(2)

    @pl.when(j == 0)
    def _():
        h = _norm_mod(x_ref[...], g_ref[...], sh_ref[...], sc_ref[...])
        h_scr[...] = h.reshape(rows, d).astype(BF16)
        acc_scr[...] = jnp.zeros(acc_scr.shape, F32)

    h = h_scr[...]
    a = _dot(h, w1_ref[...])
    b = _dot(h, w3_ref[...])
    u = (a * jax.nn.sigmoid(a) * b).astype(BF16)
    acc_scr[...] += _dot(u, w2_ref[...])

    @pl.when(j == pl.num_programs(2) - 1)
    def _():
        o_ref[...] = x_ref[...] + gate_ref[...] * acc_scr[...].reshape(bb, tt, d)


def _ffn(x, g, shift, scale, gate, w1, w3, w2, th, prompt_tile):
    nb, r, d = x.shape
    hid = w1.shape[1]
    bb, tt = _tok_tiles(x, prompt_tile)
    mod = pl.BlockSpec((bb, 1, d), lambda b, t, j: (b, 0, 0))
    return pl.pallas_call(
        _ffn_body,
        grid=(nb // bb, r // tt, hid // th),
        in_specs=[pl.BlockSpec((bb, tt, d), lambda b, t, j: (b, t, 0)),
                  pl.BlockSpec((1, 1, d), lambda b, t, j: (0, 0, 0)),
                  mod, mod, mod,
                  pl.BlockSpec((d, th), lambda b, t, j: (0, j)),
                  pl.BlockSpec((d, th), lambda b, t, j: (0, j)),
                  pl.BlockSpec((th, d), lambda b, t, j: (j, 0))],
        out_specs=pl.BlockSpec((bb, tt, d), lambda b, t, j: (b, t, 0)),
        out_shape=jax.ShapeDtypeStruct((nb, r, d), F32),
        scratch_shapes=[pltpu.VMEM((bb * tt, d), BF16), pltpu.VMEM((bb * tt, d), F32)],
        compiler_params=_cparams(("parallel", "parallel", "arbitrary")),
        name="ffn",
    )(x, g.reshape(1, 1, d), shift, scale, gate, w1, w3, w2)


def _softmax_update(s, m_scr, l_scr):
    m_old = m_scr[...]
    m_new = jnp.maximum(m_old, jnp.max(s, axis=-1, keepdims=True))
    corr = jnp.exp(m_old - m_new)
    p = jnp.exp(s - m_new)
    l_scr[...] = l_scr[...] * corr + jnp.sum(p, axis=-1, keepdims=True)
    m_scr[...] = m_new
    return p, corr


def _mla_decode_body(pt_ref, a_ref, ape_ref, ckv_ref, kpet_ref, wuk_ref, wuv_ref, *rest, pages, per):
    lat_refs = rest[:pages]
    pet_refs = rest[pages:2 * pages]
    o_ref = rest[2 * pages]
    aq_scr, apq_scr, m_scr, l_scr, acc_scr, latn_scr = rest[2 * pages + 1:]
    nh = MLA_HEADS
    s_tok = a_ref.shape[1]
    nrow = nh * s_tok
    c = pl.program_id(1)
    dk = MLA_NOPE + MLA_ROPE

    @pl.when(c == 0)
    def _():
        m_scr[...] = jnp.full(m_scr.shape, NEG_INF, F32)
        l_scr[...] = jnp.zeros(l_scr.shape, F32)
        acc_scr[...] = jnp.zeros(acc_scr.shape, F32)
        for hh in range(nh):
            aq_scr[s_tok * hh:s_tok * (hh + 1), :] = a_ref[0, :, 2 * LANES * hh:2 * LANES * (hh + 1)].astype(BF16)
            apq_scr[s_tok * hh:s_tok * (hh + 1), :] = ape_ref[0, :, LANES * hh:LANES * hh + MLA_ROPE].astype(BF16)

    def scores(latb, pet):
        nk = latb.shape[0]
        kt = _dot_t(wuk_ref[...], latb)
        pe_sq = jnp.sum(pet * pet, axis=0, keepdims=True)
        ssq = []
        for hh in range(nh):
            kh = kt[MLA_NOPE * hh:MLA_NOPE * (hh + 1), :]
            ssq.append(jnp.broadcast_to(jnp.sum(kh * kh, axis=0, keepdims=True) + pe_sq, (s_tok, nk)))
        r = _rms_scale(jnp.concatenate(ssq, axis=0), dk)
        return (_dot_t(aq_scr[...], latb) + _dot(apq_scr[...], pet.astype(BF16))) * r

    def absorb(s, latb):
        p, corr = _softmax_update(s, m_scr, l_scr)
        acc_scr[...] = acc_scr[...] * corr + _dot(p.astype(BF16), latb)

    n_sub = pages // per
    latb = [jnp.concatenate([lat_refs[u * per + w][0] for w in range(per)], axis=0).astype(BF16) for u in range(n_sub)]
    pet = [jnp.concatenate([pet_refs[u * per + w][0] for w in range(per)], axis=1) for u in range(n_sub)]
    absorb(jnp.concatenate([scores(latb[u], pet[u]) for u in range(n_sub)], axis=1), jnp.concatenate(latb, axis=0))

    @pl.when(c == pl.num_programs(1) - 1)
    def _():
        latn_scr[...] = jnp.zeros(latn_scr.shape, F32)
        latn_scr[0:s_tok, :] = ckv_ref[0]
        qi = lax.broadcasted_iota(jnp.int32, (nrow, LANES), 0) % s_tok
        kj = lax.broadcasted_iota(jnp.int32, (nrow, LANES), 1)
        latn = latn_scr[...].astype(BF16)
        absorb(jnp.where(kj <= qi, scores(latn, kpet_ref[0]), NEG_INF), latn)
        o_lat = (acc_scr[...] / l_scr[...]).astype(BF16)
        for hh in range(nh):
            o_ref[0, :, LANES * hh:LANES * (hh + 1)] = _dot(
                o_lat[s_tok * hh:s_tok * (hh + 1), :], wuv_ref[:, LANES * hh:LANES * (hh + 1)])


def _mla_decode(a, ape, ckv_s, kpe_s, cache_lat, cache_kpe, page_table, wuk_rows, wuv2, pages, per):
    db, s_tok, _ = a.shape
    n_pages = page_table.shape[1]
    page = cache_lat.shape[1]
    pages = min(pages, n_pages)
    nh = MLA_HEADS
    pt = page_table.reshape(-1)
    cache_pet = jnp.swapaxes(cache_kpe, 1, 2)
    kpet_s = jnp.pad(jnp.swapaxes(kpe_s, 1, 2), ((0, 0), (0, 0), (0, LANES - s_tok)))

    def page_spec(shape, jj):
        return pl.BlockSpec((1,) + shape, lambda b, c, pt_ref: (pt_ref[b * n_pages + c * pages + jj], 0, 0))

    seq = lambda arr: pl.BlockSpec((1,) + arr.shape[1:], lambda b, c, pt_ref: (b, 0, 0))
    const = lambda arr: pl.BlockSpec(arr.shape, lambda b, c, pt_ref: (0,) * arr.ndim)
    grid_spec = pltpu.PrefetchScalarGridSpec(
        num_scalar_prefetch=1,
        grid=(db, n_pages // pages),
        in_specs=[seq(a), seq(ape), seq(ckv_s), seq(kpet_s), const(wuk_rows), const(wuv2)]
                 + [page_spec((page, MLA_KV_RANK), jj) for jj in range(pages)]
                 + [page_spec((MLA_ROPE, page), jj) for jj in range(pages)],
        out_specs=pl.BlockSpec((1, s_tok, nh * LANES), lambda b, c, pt_ref: (b, 0, 0)),
        scratch_shapes=[pltpu.VMEM((nh * s_tok, MLA_KV_RANK), BF16), pltpu.VMEM((nh * s_tok, MLA_ROPE), BF16),
                        pltpu.VMEM((nh * s_tok, 1), F32), pltpu.VMEM((nh * s_tok, 1), F32),
                        pltpu.VMEM((nh * s_tok, MLA_KV_RANK), F32),
                        pltpu.VMEM((LANES, MLA_KV_RANK), F32)],
    )
    return pl.pallas_call(
        functools.partial(_mla_decode_body, pages=pages, per=per),
        grid_spec=grid_spec,
        out_shape=jax.ShapeDtypeStruct((db, s_tok, nh * LANES), F32),
        compiler_params=_cparams(("parallel", "arbitrary")),
        name="mla_decode",
    )(pt, a, ape, ckv_s, kpet_s, wuk_rows, wuv2, *([cache_lat] * pages), *([cache_pet] * pages))


def _diff_decode_body(pt_ref, qkv_ref, sl_ref, *rest, pages, past):
    kt_refs = rest[:pages]
    v_refs = rest[pages:2 * pages]
    o_ref = rest[2 * pages]
    qz_scr, m_scr, l_scr, acc_scr, kn_scr, vn_scr = rest[2 * pages + 1:]
    s_tok = qkv_ref.shape[1]
    nkv = DA_KV_HEADS
    per_kv = (DA_HEADS // nkv) * 2 * s_tok
    nrow = nkv * per_kv
    page = kt_refs[0].shape[2]
    c = pl.program_id(1)
    nq = DA_HEADS * 2 * DA_HEAD_DIM

    @pl.when(c == 0)
    def _():
        m_scr[...] = jnp.full(m_scr.shape, NEG_INF, F32)
        l_scr[...] = jnp.zeros(l_scr.shape, F32)
        acc_scr[...] = jnp.zeros(acc_scr.shape, F32)
        lane = lax.broadcasted_iota(jnp.int32, (s_tok, LANES), 1)
        row = 0
        for hh in range(DA_HEADS):
            qh = qkv_ref[0, :, LANES * hh:LANES * (hh + 1)]
            for mp in range(2):
                keep = (lane < DA_HEAD_DIM) if mp == 0 else (lane >= DA_HEAD_DIM)
                qz_scr[row:row + s_tok, :] = jnp.where(keep, qh, 0.0).astype(BF16)
                row += s_tok

    def absorb(s, vs, kpos0, mask):
        nk = s.shape[1]
        rel = kpos0 + lax.broadcasted_iota(jnp.int32, (1, nk), 1)
        s = s + sl_ref[...] * rel.astype(F32)
        if mask is not None:
            s = jnp.where(mask, s, NEG_INF)
        p, corr = _softmax_update(s, m_scr, l_scr)
        pb = p.astype(BF16)
        pv = jnp.concatenate([_dot(pb[per_kv * kv:per_kv * (kv + 1), :], vs[kv]) for kv in range(nkv)], axis=0)
        acc_scr[...] = acc_scr[...] * corr + pv

    def qz(kv):
        return qz_scr[per_kv * kv:per_kv * (kv + 1), :]

    kt = jnp.concatenate([kt_refs[w][0] for w in range(pages)], axis=1).astype(BF16)
    s = jnp.concatenate([_dot(qz(kv), kt[LANES * kv:LANES * (kv + 1), :]) for kv in range(nkv)], axis=0)
    vs = [jnp.concatenate([v_refs[w][0, pl.ds(kv, page, stride=nkv), :] for w in range(pages)], axis=0).astype(BF16)
          for kv in range(nkv)]
    absorb(s, vs, c * pages * page - past, None)

    @pl.when(c == pl.num_programs(1) - 1)
    def _():
        kn_scr[...] = jnp.zeros(kn_scr.shape, F32)
        vn_scr[...] = jnp.zeros(vn_scr.shape, F32)
        kn_scr[0:s_tok, :] = qkv_ref[0, :, nq:nq + nkv * LANES]
        vn_scr[0:s_tok, :] = qkv_ref[0, :, nq + nkv * LANES:nq + 2 * nkv * LANES]
        qi = lax.broadcasted_iota(jnp.int32, (nrow, LANES), 0) % s_tok
        kj = lax.broadcasted_iota(jnp.int32, (nrow, LANES), 1)
        kn = kn_scr[...].astype(BF16)
        vn = vn_scr[...].astype(BF16)
        s = jnp.concatenate([_dot_t(qz(kv), kn[:, LANES * kv:LANES * (kv + 1)]) for kv in range(nkv)], axis=0)
        absorb(s, [vn[:, LANES * kv:LANES * (kv + 1)] for kv in range(nkv)], 0, kj <= qi)
        o_ref[0] = acc_scr[...] / l_scr[...]


def _diff_decode(qkv_s, cache_k, cache_v, page_table, slopes_rows, pages):
    db, s_tok, _ = qkv_s.shape
    n_pages = page_table.shape[1]
    n_pool, page = cache_k.shape[:2]
    pages = min(pages, n_pages)
    width = DA_KV_HEADS * LANES
    nrow = DA_HEADS * 2 * s_tok
    pt = page_table.reshape(-1)
    cache_kt = jnp.transpose(cache_k, (0, 2, 3, 4, 1)).reshape(n_pool, width, page)
    cache_v2 = cache_v.reshape(n_pool, page * DA_KV_HEADS, LANES)

    def kt_spec(jj):
        return pl.BlockSpec((1, width, page), lambda b, c, pt_ref: (pt_ref[b * n_pages + c * pages + jj], 0, 0))

    def v_spec(jj):
        return pl.BlockSpec((1, page * DA_KV_HEADS, LANES), lambda b, c, pt_ref: (pt_ref[b * n_pages + c * pages + jj], 0, 0))

    grid_spec = pltpu.PrefetchScalarGridSpec(
        num_scalar_prefetch=1,
        grid=(db, n_pages // pages),
        in_specs=[pl.BlockSpec((1, s_tok, qkv_s.shape[2]), lambda b, c, pt_ref: (b, 0, 0)),
                  pl.BlockSpec((nrow, 1), lambda b, c, pt_ref: (0, 0))]
                 + [kt_spec(jj) for jj in range(pages)] + [v_spec(jj) for jj in range(pages)],
        out_specs=pl.BlockSpec((1, nrow, LANES), lambda b, c, pt_ref: (b, 0, 0)),
        scratch_shapes=[pltpu.VMEM((nrow, LANES), BF16), pltpu.VMEM((nrow, 1), F32), pltpu.VMEM((nrow, 1), F32),
                        pltpu.VMEM((nrow, LANES), F32),
                        pltpu.VMEM((LANES, width), F32), pltpu.VMEM((LANES, width), F32)],
    )
    return pl.pallas_call(
        functools.partial(_diff_decode_body, pages=pages, past=n_pages * page),
        grid_spec=grid_spec,
        out_shape=jax.ShapeDtypeStruct((db, nrow, LANES), F32),
        compiler_params=_cparams(("parallel", "arbitrary")),
        name="diff_decode",
    )(pt, qkv_s, slopes_rows, *([cache_kt] * pages), *([cache_v2] * pages))


def _dil_decode_body(qkv_ref, sl_ref, buf_ref, shifted_ref, o_ref, lse_ref, roll_ref, kn_scr, vn_scr, new_scr, sem,
                     *, dil, n_res):
    s_tok = qkv_ref.shape[1]
    nh = DL_HEADS
    hd = nh * DL_HEAD_DIM
    nrow = nh * s_tok
    per_entry = 2 * nh
    rows = buf_ref.shape[1]
    nb = rows // (per_entry * dil)
    b = pl.program_id(0)

    del shifted_ref
    keep = rows - s_tok * per_entry
    for kv in range(2):
        for hh in range(nh):
            lo = hd * (1 + kv) + DL_HEAD_DIM * hh
            new_scr[pl.ds(kv * nh + hh, s_tok, stride=per_entry), :] = qkv_ref[0, :, lo:lo + DL_HEAD_DIM]
    append = pltpu.make_async_copy(new_scr, roll_ref.at[b, pl.ds(keep, s_tok * per_entry)], sem.at[0])
    append.start()

    q = qkv_ref[0, :, :hd]
    row_h = lax.broadcasted_iota(jnp.int32, (nrow, hd), 0) // s_tok
    col_h = lax.broadcasted_iota(jnp.int32, (nrow, hd), 1) // DL_HEAD_DIM
    qbd = jnp.where(row_h == col_h, jnp.concatenate([q] * nh, axis=0), 0.0).astype(BF16)
    kn_scr[...] = jnp.zeros(kn_scr.shape, F32)
    vn_scr[...] = jnp.zeros(vn_scr.shape, F32)
    kn_scr[0:s_tok, :] = qkv_ref[0, :, hd:2 * hd]
    vn_scr[0:s_tok, :] = qkv_ref[0, :, 2 * hd:3 * hd]

    qi = lax.broadcasted_iota(jnp.int32, (nrow, nb), 0) % s_tok
    kj = lax.broadcasted_iota(jnp.int32, (nrow, nb), 1)
    q_res, q_t = qi % dil, qi // dil
    slope = sl_ref[...]

    def gather(r, kv):
        return jnp.concatenate(
            [buf_ref[0, pl.ds((r * 2 + kv) * nh + hh, nb, stride=per_entry * dil), :] for hh in range(nh)],
            axis=-1).astype(BF16)

    blocks = []
    for r in range(n_res):
        steps = nb + q_t - kj
        s = _dot_t(qbd, gather(r, 0)) - slope * (steps * dil).astype(F32)
        blocks.append((jnp.where((q_res == r) & (kj >= q_t), s, NEG_INF), gather(r, 1)))
    s = _dot_t(qbd, kn_scr[...].astype(BF16)) - slope * (qi - kj).astype(F32)
    ok = (kj < s_tok) & (kj <= qi) & (kj % dil == q_res)
    blocks.append((jnp.where(ok, s, NEG_INF), vn_scr[...].astype(BF16)))

    m = blocks[0][0].max(axis=-1, keepdims=True)
    for s, _ in blocks[1:]:
        m = jnp.maximum(m, s.max(axis=-1, keepdims=True))
    l = jnp.zeros((nrow, 1), F32)
    acc = jnp.zeros((nrow, hd), F32)
    for s, v in blocks:
        p = jnp.exp(s - m)
        l = l + jnp.sum(p, axis=-1, keepdims=True)
        acc = acc + _dot(p.astype(BF16), v)
    out = acc / l
    lse = m + jnp.log(l)
    for hh in range(nh):
        rs = slice(s_tok * hh, s_tok * (hh + 1))
        cs = slice(DL_HEAD_DIM * hh, DL_HEAD_DIM * (hh + 1))
        o_ref[0, :, cs] = out[rs, cs]
        lse_ref[0, :, cs] = jnp.broadcast_to(lse[rs], (s_tok, DL_HEAD_DIM))
    append.wait()


def _dil_decode(qkv_s, buf, shifted2, slopes_rows, g, dil):
    db, s_tok, _ = qkv_s.shape
    nh = DL_HEADS
    hd = nh * DL_HEAD_DIM
    wb = buf.shape[1]
    assert wb == N_BACK * dil, "window buffer must hold the whole window"
    n_res = min(dil, s_tok)
    rows = wb * 2 * nh
    buf2 = buf.reshape(db, rows, DL_HEAD_DIM)
    out = jax.ShapeDtypeStruct((db, s_tok, hd), F32)
    o_g, lse_g, rolled = pl.pallas_call(
        functools.partial(_dil_decode_body, dil=dil, n_res=n_res),
        grid=(db,),
        in_specs=[pl.BlockSpec((1, s_tok, 3 * hd), lambda b: (b, 0, g)),
                  pl.BlockSpec((nh * s_tok, 1), lambda b: (0, 0)),
                  pl.BlockSpec((1, rows, DL_HEAD_DIM), lambda b: (b, 0, 0)),
                  pl.BlockSpec(memory_space=pl.ANY)],
        out_specs=[pl.BlockSpec((1, s_tok, hd), lambda b: (b, 0, 0))] * 2 + [pl.BlockSpec(memory_space=pl.ANY)],
        out_shape=[out, out, jax.ShapeDtypeStruct(buf2.shape, buf.dtype)],
        input_output_aliases={3: 2},
        scratch_shapes=[pltpu.VMEM((LANES, hd), F32), pltpu.VMEM((LANES, hd), F32),
                        pltpu.VMEM((s_tok * 2 * nh, DL_HEAD_DIM), F32), pltpu.SemaphoreType.DMA((1,))],
        compiler_params=_cparams(("arbitrary",)),
        name="dil_decode",
    )(qkv_s, slopes_rows, buf2, shifted2)
    return o_g, lse_g, rolled.reshape(buf.shape)


PROMPT_TILE = 512
FFN_TILE = 1024
FFN_HIDDEN_TILE = 256
DECODE_PAGES = 8
MLA_DECODE_PAGES = 16
DECODE_SUB_PAGES = 2
FLASH_ROW_CHUNK = 256


def _alibi_slopes(n):
    return np.array([2.0 ** (-8.0 * (i + 1) / n) for i in range(n)], dtype=np.float32)


def _mla_layer(hp_args, hs_args, caches, page_table, params):
    w_dqkv, g_q, g_kv, w_uq, w_uk, w_uv, q_norm, k_norm, w_o = params
    xp, gmix, shp, scp = hp_args
    xs, _, shs, scs = hs_args
    cache_lat, cache_kpe = caches
    wts = _mla_weights(w_dqkv, w_uq, w_uk, w_uv, q_norm, k_norm)
    b, t, _ = xp.shape
    db, s_tok, _ = xs.shape
    past = page_table.shape[1] * cache_lat.shape[1]
    ckv_p, kpe_p, q, k, v = _mla_proj(xp, gmix, shp, scp, jnp.arange(t), wts, g_q, g_kv, False, PROMPT_TILE)
    nh = MLA_HEADS
    o_p = _flash(q, k, v, jnp.zeros((nh, min(2048, t), LANES), F32), n_groups=nh, n_stack=1,
                 dq=2 * LANES, dk=2 * LANES, dv=LANES, q_col=0, k_col=0, v_col=0,
                 tq=2048, tk=1024, alibi=False, split_maps=False, out_dtype=BF16)
    ckv_s, kpe_s, a, ape = _mla_proj(xs, gmix, shs, scs, past + jnp.arange(s_tok), wts, g_q, g_kv, True, PROMPT_TILE)
    wuk_rows = wts[3].reshape(nh * MLA_NOPE, MLA_KV_RANK)
    o_s = _mla_decode(a, ape, ckv_s, kpe_s, cache_lat, cache_kpe, page_table, wuk_rows, wts[4],
                      MLA_DECODE_PAGES, DECODE_SUB_PAGES)
    return (o_p,), (o_s,), "plain", (), w_o, (ckv_p, ckv_s, kpe_p, kpe_s)


def _diff_layer(hp_args, hs_args, caches, page_table, layer_idx, params, next_windows):
    w_qkv, q_norm, k_norm, lq1, lk1, lq2, lk2, subln, w_o = params
    xp, gmix, shp, scp = hp_args
    xs, _, shs, scs = hs_args
    cache_k, cache_v = caches
    b, t, _ = xp.shape
    db, s_tok, _ = xs.shape
    scale = DA_HEAD_DIM ** -0.5
    tn = 512
    rep = tn // DA_HEAD_DIM
    gains = jnp.stack([jnp.tile(q_norm, rep) * scale, jnp.tile(k_norm, rep), jnp.ones((tn,), F32)]).reshape(3, 1, tn)
    nq_tiles = DA_HEADS * 2 * DA_HEAD_DIM // tn
    nk_tiles = DA_KV_HEADS * 2 * DA_HEAD_DIM // tn
    gain_idx = lambda j: jnp.maximum(j - (nq_tiles - 1), 0)
    kind = lambda j: j >= nq_tiles + nk_tiles
    wq = w_qkv.astype(BF16)
    qkv_p = _group_proj(xp, gmix, shp, scp, wq, gains, gain_idx, kind, DA_HEAD_DIM, tn, FFN_TILE)
    qkv_s = _group_proj(xs, gmix, shs, scs, wq, gains, gain_idx, kind, DA_HEAD_DIM, tn, PROMPT_TILE)
    slopes = _alibi_slopes(DA_HEADS)
    tq = min(512, t)
    per_kv = DA_HEADS // DA_KV_HEADS
    sl_rows = np.repeat(slopes.reshape(DA_KV_HEADS, per_kv), 2 * tq, axis=1).reshape(DA_KV_HEADS, per_kv * 2 * tq, 1)
    sl_feat = np.zeros(sl_rows.shape[:2] + (LANES,), np.float32)
    sl_feat[:, :, 0:1] = sl_rows * LANES
    sl_feat[:, :, 1:2] = sl_rows
    assert np.array_equal(sl_feat, sl_feat.astype(BF16).astype(np.float32)), "slopes must be exact in bf16"
    kc = DA_HEADS * 2 * DA_HEAD_DIM // LANES
    o_p = _flash(qkv_p, qkv_p, qkv_p, jnp.asarray(sl_feat), n_groups=DA_KV_HEADS, n_stack=per_kv * 2,
                 dq=per_kv * LANES, dk=LANES, dv=LANES, q_col=0, k_col=kc, v_col=kc + DA_KV_HEADS,
                 tq=tq, tk=1024, alibi=True, split_maps=True, out_dtype=BF16,
                 side=tuple(w.reshape(w.shape[0], -1, DL_HEAD_DIM) for w in next_windows),
                 side_lead=s_tok * 2 * DL_HEADS)
    if next_windows:
        o_p, shifted = o_p
    else:
        shifted = ()
    sl_dec = jnp.asarray(np.repeat(slopes, 2 * s_tok).reshape(DA_HEADS * 2 * s_tok, 1))
    o_dec = _diff_decode(qkv_s, cache_k, cache_v, page_table, sl_dec, DECODE_PAGES)
    o_s = jnp.transpose(o_dec.reshape(db, DA_HEADS * 2, s_tok, LANES), (0, 2, 1, 3)).reshape(db, s_tok, DA_HEADS * 2 * LANES)
    nq = DA_HEADS * 2 * DA_HEAD_DIM
    nk = DA_KV_HEADS * 2 * DA_HEAD_DIM
    k_p = qkv_p[..., nq:nq + nk].reshape(b, t, DA_KV_HEADS, 2, DA_HEAD_DIM)
    k_s = qkv_s[..., nq:nq + nk].reshape(db, s_tok, DA_KV_HEADS, 2, DA_HEAD_DIM)
    v_p = qkv_p[..., nq + nk:].reshape(b, t, DA_KV_HEADS, 2 * DA_HEAD_DIM)
    v_s = qkv_s[..., nq + nk:].reshape(db, s_tok, DA_KV_HEADS, 2 * DA_HEAD_DIM)
    extras = tuple(a.reshape(1, -1) for a in (lq1, lk1, lq2, lk2, subln))
    return (o_p,), (o_s,), "diff", extras, w_o, (k_p, k_s, v_p, v_s), shifted


def _dilated_layer(hp_args, hs_args, caches, shifted, params):
    w_qkv, q_norm, k_norm, w_o = params
    xp, gmix, shp, scp = hp_args
    xs, _, shs, scs = hs_args
    b, t, _ = xp.shape
    db, s_tok, _ = xs.shape
    nh, hd = DL_HEADS, DL_HEAD_DIM
    scale = hd ** -0.5
    tn = nh * hd
    gains = jnp.stack([jnp.tile(q_norm, nh) * scale, jnp.tile(k_norm, nh), jnp.ones((tn,), F32)]).reshape(3, 1, tn)
    wq = w_qkv.astype(BF16)
    gain_idx = lambda j: j % 3
    kind = lambda j: j % 3 == 2
    qkv_p = _group_proj(xp, gmix, shp, scp, wq, gains, gain_idx, kind, hd, tn, FFN_TILE)
    qkv_s = _group_proj(xs, gmix, shs, scs, wq, gains, gain_idx, kind, hd, tn, PROMPT_TILE)
    slopes = _alibi_slopes(nh)
    sl_lane = jnp.asarray(np.repeat(slopes, LANES).reshape(nh, 1, LANES))
    o_p = _band_attention(qkv_p, sl_lane, 2048, BF16)
    sl_rows = jnp.asarray(np.repeat(slopes, s_tok).reshape(nh * s_tok, 1))
    outs, lses, st = [], [], []
    qkv_p6 = qkv_p.reshape(b, t, len(DL_CONFIGS), 3, nh, hd)
    for g, (win, dil) in enumerate(DL_CONFIGS):
        o_g, lse_g, rolled = _dil_decode(qkv_s, caches[g], shifted[g], sl_rows, g, dil)
        outs.append(o_g)
        lses.append(lse_g)
        wb = min(win, t)
        st.append(qkv_p6[:, t - wb:, g, 1:3])
        st.append(rolled)
    return (o_p,), tuple(outs) + tuple(lses), ("plain", "dil"), (), w_o, tuple(st)


def kernel(x_prompt, x_sample, cache_l0_latent, cache_l0_kpe, cache_l1_k, cache_l1_v, cache_l2_kv_w128, cache_l2_kv_w512, cache_l2_kv_w2048, cache_l3_latent, cache_l3_kpe, page_table, c_prompt, c_sample, l0_ada_w, l0_ada_b, l0_norm_mix, l0_w_dqkv, l0_g_q, l0_g_kv, l0_w_uq, l0_w_uk, l0_w_uv, l0_q_norm, l0_k_norm, l0_w_o, l0_norm_ffn, l0_ffn_w1, l0_ffn_w3, l0_ffn_w2, l1_ada_w, l1_ada_b, l1_norm_mix, l1_w_qkv, l1_q_norm, l1_k_norm, l1_lambda_q1, l1_lambda_k1, l1_lambda_q2, l1_lambda_k2, l1_subln, l1_w_o, l1_norm_ffn, l1_ffn_w1, l1_ffn_w3, l1_ffn_w2, l2_ada_w, l2_ada_b, l2_norm_mix, l2_w_qkv, l2_q_norm, l2_k_norm, l2_w_o, l2_norm_ffn, l2_ffn_w1, l2_ffn_w3, l2_ffn_w2, l3_ada_w, l3_ada_b, l3_norm_mix, l3_w_dqkv, l3_g_q, l3_g_kv, l3_w_uq, l3_w_uk, l3_w_uv, l3_q_norm, l3_k_norm, l3_w_o, l3_norm_ffn, l3_ffn_w1, l3_ffn_w3, l3_ffn_w2):
    block = [
        (l0_ada_w, l0_ada_b, l0_norm_mix, l0_norm_ffn, l0_ffn_w1, l0_ffn_w3, l0_ffn_w2),
        (l1_ada_w, l1_ada_b, l1_norm_mix, l1_norm_ffn, l1_ffn_w1, l1_ffn_w3, l1_ffn_w2),
        (l2_ada_w, l2_ada_b, l2_norm_mix, l2_norm_ffn, l2_ffn_w1, l2_ffn_w3, l2_ffn_w2),
        (l3_ada_w, l3_ada_b, l3_norm_mix, l3_norm_ffn, l3_ffn_w1, l3_ffn_w3, l3_ffn_w2),
    ]
    mixer = [
        (l0_w_dqkv, l0_g_q, l0_g_kv, l0_w_uq, l0_w_uk, l0_w_uv, l0_q_norm, l0_k_norm, l0_w_o),
        (l1_w_qkv, l1_q_norm, l1_k_norm, l1_lambda_q1, l1_lambda_k1, l1_lambda_q2, l1_lambda_k2, l1_subln, l1_w_o),
        (l2_w_qkv, l2_q_norm, l2_k_norm, l2_w_o),
        (l3_w_dqkv, l3_g_q, l3_g_kv, l3_w_uq, l3_w_uk, l3_w_uv, l3_q_norm, l3_k_norm, l3_w_o),
    ]
    caches = [
        (cache_l0_latent, cache_l0_kpe),
        (cache_l1_k, cache_l1_v),
        (cache_l2_kv_w128, cache_l2_kv_w512, cache_l2_kv_w2048),
        (cache_l3_latent, cache_l3_kpe),
    ]
    b, t, d = x_prompt.shape
    db, s_tok, _ = x_sample.shape
    rows = b + db
    rows_pad = -(-rows // 8) * 8
    c_all = jnp.pad(jnp.concatenate([c_prompt, c_sample], axis=0), ((0, rows_pad - rows), (0, 0)))
    xp, xs = x_prompt, x_sample
    states = []
    for i in range(len(block)):
        ada_w, ada_b, g_mix, g_ffn, w1, w3, w2 = block[i]
        mod = _adaln(c_all, ada_w, ada_b)
        mod_p = mod[:, :b].reshape(6, b, 1, d)
        mod_s = mod[:, b:rows].reshape(6, db, 1, d)
        hp_args = (xp, g_mix, mod_p[0], mod_p[1])
        hs_args = (xs, g_mix, mod_s[0], mod_s[1])
        kind = i % 3
        lam_init = 0.0
        if kind == 0:
            a_p, a_s, mode, extras, w_o, st = _mla_layer(hp_args, hs_args, caches[i], page_table, mixer[i])
        elif kind == 1:
            lam_init = 0.8 - 0.6 * math.exp(-0.3 * i)
            nxt = caches[i + 1] if i + 1 < len(block) and (i + 1) % 3 == 2 else ()
            a_p, a_s, mode, extras, w_o, st, shifted = _diff_layer(hp_args, hs_args, caches[i], page_table, i, mixer[i], nxt)
        else:
            a_p, a_s, mode, extras, w_o, st = _dilated_layer(hp_args, hs_args, caches[i], shifted, mixer[i])
        mode_p, mode_s = mode if isinstance(mode, tuple) else (mode, mode)
        wo_bf = w_o.astype(BF16)
        xp = _out_proj(a_p, extras, wo_bf, xp, mod_p[2], mode_p, lam_init, PROMPT_TILE)
        xs = _out_proj(a_s, extras, wo_bf, xs, mod_s[2], mode_s, lam_init, PROMPT_TILE)
        w1b, w3b, w2b = w1.astype(BF16), w3.astype(BF16), w2.astype(BF16)
        xp = _ffn(xp, g_ffn, mod_p[3], mod_p[4], mod_p[5], w1b, w3b, w2b, FFN_HIDDEN_TILE, FFN_TILE)
        xs = _ffn(xs, g_ffn, mod_s[3], mod_s[4], mod_s[5], w1b, w3b, w2b, FFN_HIDDEN_TILE, FFN_TILE)
        states.append(st)
    l0, l1, l2, l3 = states
    return (xp, xs, l0[0], l0[1], l0[2], l0[3], l1[0], l1[1], l1[2], l1[3],
            l2[0], l2[1], l2[2], l2[3], l2[4], l2[5], l3[0], l3[1], l3[2], l3[3])
```
